```python
import math
import jax, jax.numpy as jnp
from jax import lax
import numpy as np

D_MODEL = 1024
BATCH = 32
SEQ = 2048
DEPTH = 1

HEAD_DIM = 64
DIFF_HEADS = 4
DIFF_WIDTH = DIFF_HEADS * 2 * HEAD_DIM
NSA_HEADS = 8
NSA_KV_HEADS = 2
NSA_GROUP = NSA_HEADS // NSA_KV_HEADS
NSA_WIDTH = NSA_HEADS * HEAD_DIM
NSA_KV_WIDTH = NSA_KV_HEADS * HEAD_DIM
CMP_BLOCK = 32
CMP_STRIDE = 16
CMP_HIDDEN = 128
SEL_BLOCK = 64
SEL_TOPN = 16
WINDOW = 512
N_GATES = 3 * NSA_HEADS
Q_BLOCK = 128
SEL_Q_BLOCK = 32
MIX_WIDTH = DIFF_WIDTH + NSA_WIDTH
D_FF = -(-8 * D_MODEL // (3 * 256)) * 256
N_ALIBI = DIFF_HEADS + NSA_HEADS
LN_EPS = 1e-5
RMS_EPS = 1e-5
NEG_INF = -1e30
DEEPNORM_ALPHA = (2.0 * DEPTH) ** 0.25
DEEPNORM_BETA = (8.0 * DEPTH) ** -0.25

OFF_DQ = 0
OFF_DK = OFF_DQ + DIFF_WIDTH
OFF_DV = OFF_DK + DIFF_WIDTH
OFF_NQ = OFF_DV + DIFF_WIDTH
OFF_CK = OFF_NQ + NSA_WIDTH
OFF_CV = OFF_CK + NSA_KV_WIDTH
OFF_SK = OFF_CV + NSA_KV_WIDTH
OFF_SV = OFF_SK + NSA_KV_WIDTH
OFF_WK = OFF_SV + NSA_KV_WIDTH
OFF_WV = OFF_WK + NSA_KV_WIDTH
OFF_G = OFF_WV + NSA_KV_WIDTH
N_IN = OFF_G + N_GATES

kernel_name = "hybrid_diffattn_nsa_alibi_deepnorm"


def lambda_init(layer_idx):
    return 0.8 - 0.6 * math.exp(-0.3 * layer_idx)


def alibi_slopes():
    return jnp.asarray(2.0 ** (-8.0 * (np.arange(N_ALIBI) + 1) / N_ALIBI), dtype=jnp.float32)


def layer_norm(x, g, b):
    xf = x.astype(jnp.float32)
    mu = xf.mean(-1, keepdims=True)
    var = jnp.square(xf - mu).mean(-1, keepdims=True)
    return ((xf - mu) * lax.rsqrt(var + LN_EPS) * g.astype(jnp.float32) + b.astype(jnp.float32)).astype(x.dtype)


def rms_norm(x, g):
    xf = x.astype(jnp.float32)
    return (xf * lax.rsqrt(jnp.mean(xf * xf, -1, keepdims=True) + RMS_EPS) * g.astype(jnp.float32)).astype(x.dtype)


def masked_softmax(s, mask):
    s = jnp.where(mask, s.astype(jnp.float32), NEG_INF)
    return jax.nn.softmax(s, axis=-1) * mask


def diff_attention(q, k, v, lq1, lk1, lq2, lk2, subln_g, slopes, lam_init):
    B, T = q.shape[0], q.shape[1]
    dt = v.dtype
    f32 = jnp.float32
    lam = (jnp.exp(jnp.sum(lq1.astype(f32) * lk1.astype(f32)))
           - jnp.exp(jnp.sum(lq2.astype(f32) * lk2.astype(f32))) + lam_init)
    scale = HEAD_DIM ** -0.5
    nq = T // Q_BLOCK
    qb = q.reshape(B, nq, Q_BLOCK, DIFF_HEADS, 2, HEAD_DIM).swapaxes(0, 1)
    kpos = jnp.arange(T)

    def block(args):
        qi, q_blk = args
        qpos = qi * Q_BLOCK + jnp.arange(Q_BLOCK)
        dist = (qpos[:, None] - kpos[None, :]).astype(f32)
        s = jnp.einsum('bqhcd,bkhcd->bhcqk', q_blk, k).astype(f32) * scale
        s = s - slopes[None, :, None, None, None] * dist
        p = masked_softmax(s, dist >= 0)
        a = p[:, :, 0] - lam * p[:, :, 1]
        return jnp.einsum('bhqk,bkhe->bqhe', a.astype(dt), v)

    o = lax.map(block, (jnp.arange(nq), qb))
    o = o.swapaxes(0, 1).reshape(B, T, DIFF_HEADS, 2 * HEAD_DIM)
    o = (rms_norm(o, subln_g).astype(jnp.float32) * (1.0 - lam_init)).astype(dt)
    return o.reshape(B, T, DIFF_WIDTH)


def compress_kv(kv, pe, w1, w2):
    B, T = kv.shape[0], kv.shape[1]
    n_cmp = (T - CMP_BLOCK) // CMP_STRIDE + 1
    idx = jnp.arange(n_cmp)[:, None] * CMP_STRIDE + jnp.arange(CMP_BLOCK)[None, :]
    blk = kv[:, idx] + pe[None, None, :, None, :]
    blk = jnp.moveaxis(blk, 3, 2).reshape(B, n_cmp, NSA_KV_HEADS, CMP_BLOCK * HEAD_DIM)
    return jax.nn.gelu(blk @ w1) @ w2


def nsa_attention(q, ck, cv, sk, sv, wk, wv, gates, pe_k, w1_k, w2_k, pe_v, w1_v, w2_v, slopes):
    B, T = q.shape[0], q.shape[1]
    dt = q.dtype
    f32 = jnp.float32
    G, Hg = NSA_KV_HEADS, NSA_GROUP
    scale = HEAD_DIM ** -0.5
    t = jnp.arange(T)

    kc = compress_kv(ck, pe_k, w1_k, w2_k)
    vc = compress_kv(cv, pe_v, w1_v, w2_v)
    n_cmp = kc.shape[1]
    cmp_start = jnp.arange(n_cmp) * CMP_STRIDE
    mask_c = (cmp_start + CMP_BLOCK - 1)[None, :] <= t[:, None]
    s_c = jnp.einsum('btghd,bngd->btghn', q, kc) * scale
    p_cmp = masked_softmax(s_c, mask_c[None, :, None, None, :])
    o_cmp = jnp.einsum('btghn,bngd->btghd', p_cmp.astype(dt), vc)

    n_sel = T // SEL_BLOCK
    sel_start = jnp.arange(n_sel) * SEL_BLOCK
    overlap = ((cmp_start[:, None] <= sel_start[None, :] + SEL_BLOCK - 1)
               & (cmp_start[:, None] + CMP_BLOCK - 1 >= sel_start[None, :])).astype(f32)
    imp = jnp.einsum('btghn,ns->btgs', p_cmp, overlap)
    cur = t // SEL_BLOCK
    blk = jnp.arange(n_sel)
    forced = (blk[None, :] == 0) | (blk[None, :] == cur[:, None]) | (blk[None, :] == cur[:, None] - 1)
    future = blk[None, :] > cur[:, None]
    imp = jnp.where(forced[None, :, None, :], jnp.inf,
                    jnp.where(future[None, :, None, :], -jnp.inf, imp))
    top_n = min(SEL_TOPN, n_sel)
    _, sel_idx = lax.top_k(imp, top_n)

    ks_blk = sk.reshape(B, n_sel, SEL_BLOCK, G, HEAD_DIM).transpose(0, 3, 1, 2, 4)
    vs_blk = sv.reshape(B, n_sel, SEL_BLOCK, G, HEAD_DIM).transpose(0, 3, 1, 2, 4)
    nqc = T // SEL_Q_BLOCK
    qc = q.reshape(B, nqc, SEL_Q_BLOCK, G, Hg, HEAD_DIM).swapaxes(0, 1)
    ic = sel_idx.reshape(B, nqc, SEL_Q_BLOCK, G, top_n).swapaxes(0, 1)
    b_ix = jnp.arange(B)[:, None, None, None]
    g_ix = jnp.arange(G)[None, None, :, None]
    n_keys = top_n * SEL_BLOCK

    def sel_block(args):
        ci, q_blk, i_blk = args
        kg = ks_blk[b_ix, g_ix, i_blk].reshape(B, SEL_Q_BLOCK, G, n_keys, HEAD_DIM)
        vg = vs_blk[b_ix, g_ix, i_blk].reshape(B, SEL_Q_BLOCK, G, n_keys, HEAD_DIM)
        qpos = ci * SEL_Q_BLOCK + jnp.arange(SEL_Q_BLOCK)
        kpos = (i_blk[..., None] * SEL_BLOCK + jnp.arange(SEL_BLOCK)).reshape(B, SEL_Q_BLOCK, G, n_keys)
        dist = (qpos[None, :, None, None] - kpos).astype(f32)
        s = jnp.einsum('bqghd,bqgkd->bqghk', q_blk, kg).astype(f32) * scale
        s = s - slopes[None, None, :, :, None] * dist[:, :, :, None, :]
        p = masked_softmax(s, (dist >= 0)[:, :, :, None, :])
        return jnp.einsum('bqghk,bqgkd->bqghd', p.astype(dt), vg)

    o_sel = lax.map(sel_block, (jnp.arange(nqc), qc, ic))
    o_sel = o_sel.swapaxes(0, 1).reshape(B, T, G, Hg, HEAD_DIM)

    kwp = jnp.pad(wk, ((0, 0), (WINDOW, 0), (0, 0), (0, 0)))
    vwp = jnp.pad(wv, ((0, 0), (WINDOW, 0), (0, 0), (0, 0)))
    nq = T // Q_BLOCK
    qb = q.reshape(B, nq, Q_BLOCK, G, Hg, HEAD_DIM).swapaxes(0, 1)
    span = WINDOW + Q_BLOCK

    def win_block(args):
        qi, q_blk = args
        start = qi * Q_BLOCK
        kblk = lax.dynamic_slice_in_dim(kwp, start, span, axis=1)
        vblk = lax.dynamic_slice_in_dim(vwp, start, span, axis=1)
        qpos = start + jnp.arange(Q_BLOCK)
        kpos = start - WINDOW + jnp.arange(span)
        dist_i = qpos[:, None] - kpos[None, :]
        mask = (dist_i >= 0) & (dist_i < WINDOW) & (kpos >= 0)[None, :]
        dist = dist_i.astype(f32)
        s = jnp.einsum('bqghd,bkgd->bqghk', q_blk, kblk).astype(f32) * scale
        s = s - slopes[None, None, :, :, None] * dist[None, :, None, None, :]
        p = masked_softmax(s, mask[None, :, None, None, :])
        return jnp.einsum('bqghk,bkgd->bqghd', p.astype(dt), vblk)

    o_win = lax.map(win_block, (jnp.arange(nq), qb))
    o_win = o_win.swapaxes(0, 1).reshape(B, T, G, Hg, HEAD_DIM)

    o = gates[..., 0:1] * o_cmp + gates[..., 1:2] * o_sel + gates[..., 2:3] * o_win
    return o.reshape(B, T, NSA_WIDTH)


def setup_inputs(seed: int = 0) -> dict:
    key = jax.random.key(seed)
    ks = jax.random.split(key, 24)
    f32 = jnp.float32
    nrm = lambda k, shape, s: jax.random.normal(k, shape, f32) * s
    L = DEPTH
    col_scale = np.ones((N_IN,), np.float32)
    col_scale[OFF_DV:OFF_DV + DIFF_WIDTH] = DEEPNORM_BETA
    for off in (OFF_CV, OFF_SV, OFF_WV):
        col_scale[off:off + NSA_KV_WIDTH] = DEEPNORM_BETA
    w_in = nrm(ks[1], (L, D_MODEL, N_IN), D_MODEL ** -0.5) * jnp.asarray(col_scale)
    return {
        "x": nrm(ks[0], (BATCH, SEQ, D_MODEL), 1.0),
        "w_in": w_in,
        "diff_lq1": nrm(ks[2], (L, HEAD_DIM), 0.1),
        "diff_lk1": nrm(ks[3], (L, HEAD_DIM), 0.1),
        "diff_lq2": nrm(ks[4], (L, HEAD_DIM), 0.1),
        "diff_lk2": nrm(ks[5], (L, HEAD_DIM), 0.1),
        "diff_subln_g": 1.0 + nrm(ks[6], (L, 2 * HEAD_DIM), 0.02),
        "cmp_pe_k": nrm(ks[7], (L, CMP_BLOCK, HEAD_DIM), 0.02),
        "cmp_w1_k": nrm(ks[8], (L, CMP_BLOCK * HEAD_DIM, CMP_HIDDEN), (CMP_BLOCK * HEAD_DIM) ** -0.5),
        "cmp_w2_k": nrm(ks[9], (L, CMP_HIDDEN, HEAD_DIM), CMP_HIDDEN ** -0.5),
        "cmp_pe_v": nrm(ks[10], (L, CMP_BLOCK, HEAD_DIM), 0.02),
        "cmp_w1_v": nrm(ks[11], (L, CMP_BLOCK * HEAD_DIM, CMP_HIDDEN), (CMP_BLOCK * HEAD_DIM) ** -0.5),
        "cmp_w2_v": nrm(ks[12], (L, CMP_HIDDEN, HEAD_DIM), CMP_HIDDEN ** -0.5),
        "w_out": nrm(ks[13], (L, MIX_WIDTH, D_MODEL), MIX_WIDTH ** -0.5 * DEEPNORM_BETA),
        "ln1_g": 1.0 + nrm(ks[14], (L, D_MODEL), 0.02),
        "ln1_b": nrm(ks[15], (L, D_MODEL), 0.02),
        "w_gate": nrm(ks[16], (L, D_MODEL, D_FF), D_MODEL ** -0.5),
        "w_up": nrm(ks[17], (L, D_MODEL, D_FF), D_MODEL ** -0.5),
        "w_down": nrm(ks[18], (L, D_FF, D_MODEL), D_FF ** -0.5 * DEEPNORM_BETA),
        "ln2_g": 1.0 + nrm(ks[19], (L, D_MODEL), 0.02),
        "ln2_b": nrm(ks[20], (L, D_MODEL), 0.02),
    }


def reference(x, w_in, diff_lq1, diff_lk1, diff_lq2, diff_lk2, diff_subln_g,
              cmp_pe_k, cmp_w1_k, cmp_w2_k, cmp_pe_v, cmp_w1_v, cmp_w2_v,
              w_out, ln1_g, ln1_b, w_gate, w_up, w_down, ln2_g, ln2_b):
    B, T, _ = x.shape
    slopes = alibi_slopes()
    diff_slopes = slopes[:DIFF_HEADS]
    nsa_slopes = slopes[DIFF_HEADS:].reshape(NSA_KV_HEADS, NSA_GROUP)
    for l in range(DEPTH):
        h = x @ w_in[l]
        dq = h[..., OFF_DQ:OFF_DK].reshape(B, T, DIFF_HEADS, 2, HEAD_DIM)
        dk = h[..., OFF_DK:OFF_DV].reshape(B, T, DIFF_HEADS, 2, HEAD_DIM)
        dv = h[..., OFF_DV:OFF_NQ].reshape(B, T, DIFF_HEADS, 2 * HEAD_DIM)
        nq = h[..., OFF_NQ:OFF_CK].reshape(B, T, NSA_KV_HEADS, NSA_GROUP, HEAD_DIM)
        kv = lambda o: h[..., o:o + NSA_KV_WIDTH].reshape(B, T, NSA_KV_HEADS, HEAD_DIM)
        gates = jax.nn.sigmoid(h[..., OFF_G:N_IN].reshape(B, T, NSA_KV_HEADS, NSA_GROUP, 3))
        o_diff = diff_attention(dq, dk, dv, diff_lq1[l], diff_lk1[l], diff_lq2[l], diff_lk2[l],
                                diff_subln_g[l], diff_slopes, lambda_init(l))
        o_nsa = nsa_attention(nq, kv(OFF_CK), kv(OFF_CV), kv(OFF_SK), kv(OFF_SV), kv(OFF_WK), kv(OFF_WV),
                              gates, cmp_pe_k[l], cmp_w1_k[l], cmp_w2_k[l],
                              cmp_pe_v[l], cmp_w1_v[l], cmp_w2_v[l], nsa_slopes)
        mix = jnp.concatenate([o_diff, o_nsa], axis=-1) @ w_out[l]
        x = layer_norm(DEEPNORM_ALPHA * x + mix, ln1_g[l], ln1_b[l])
        ffn = (jax.nn.silu(x @ w_gate[l]) * (x @ w_up[l])) @ w_down[l]
        x = layer_norm(DEEPNORM_ALPHA * x + ffn, ln2_g[l], ln2_b[l])
    return x
```

```python
import functools
import math

import numpy as np
import jax
import jax.numpy as jnp
from jax import lax
from jax.experimental import pallas as pl
from jax.experimental.pallas import tpu as pltpu

F32 = jnp.float32
BF16 = jnp.bfloat16

D_MODEL = 1024
HEAD_DIM = 64
DIFF_HEADS = 4
DIFF_WIDTH = DIFF_HEADS * 2 * HEAD_DIM
NSA_HEADS = 8
NSA_KV_HEADS = 2
NSA_GROUP = NSA_HEADS // NSA_KV_HEADS
NSA_WIDTH = NSA_HEADS * HEAD_DIM
NSA_KV_WIDTH = NSA_KV_HEADS * HEAD_DIM
CMP_BLOCK = 32
CMP_STRIDE = 16
CMP_HIDDEN = 128
SEL_BLOCK = 64
SEL_TOPN = 16
WINDOW = 512
N_GATES = 3 * NSA_HEADS
D_FF = 2816
N_ALIBI = DIFF_HEADS + NSA_HEADS
LN_EPS = 1e-5
RMS_EPS = 1e-5
NEG_INF = -1e30
DEPTH = 1
DEEPNORM_ALPHA = (2.0 * DEPTH) ** 0.25
SCALE = HEAD_DIM ** -0.5

OFF_DQ = 0
OFF_DK = OFF_DQ + DIFF_WIDTH
OFF_DV = OFF_DK + DIFF_WIDTH
OFF_NQ = OFF_DV + DIFF_WIDTH
OFF_CK = OFF_NQ + NSA_WIDTH
OFF_CV = OFF_CK + NSA_KV_WIDTH
OFF_SK = OFF_CV + NSA_KV_WIDTH
OFF_SV = OFF_SK + NSA_KV_WIDTH
OFF_WK = OFF_SV + NSA_KV_WIDTH
OFF_WV = OFF_WK + NSA_KV_WIDTH
OFF_G = OFF_WV + NSA_KV_WIDTH
N_IN = OFF_G + N_GATES

LANES = 128
N_IN_PAD = -(-N_IN // LANES) * LANES
VMEM_LIMIT = 56 * 1024 * 1024

PROJ_TM = 512
PROJ_TN = 256
DIFF_TQ = 256
CMP_TQ = 256
NSA_TQ = 128
FFN_TM = 512
FFN_TF = 256


def _lambda_init(layer_idx):
    return 0.8 - 0.6 * math.exp(-0.3 * layer_idx)


def _alibi_slopes():
    return np.asarray(2.0 ** (-8.0 * (np.arange(N_ALIBI) + 1) / N_ALIBI), dtype=np.float32)


def _params(*sem):
    return pltpu.CompilerParams(dimension_semantics=sem, vmem_limit_bytes=VMEM_LIMIT)


def _in_proj_kernel(x_ref, w_ref, h_ref, hck_ref, hcv_ref):
    xb = x_ref[...].astype(BF16)
    for c0 in range(0, N_IN_PAD, PROJ_TN):
        c1 = min(c0 + PROJ_TN, N_IN_PAD)
        r = jnp.dot(xb, w_ref[:, c0:c1], preferred_element_type=F32)
        if c0 < OFF_DK or OFF_NQ <= c0 < OFF_CK:
            r = r * SCALE
        h_ref[:, c0:c1] = r.astype(BF16)
        if c0 == OFF_CK:
            hck_ref[...] = r[:, :NSA_KV_WIDTH]
            hcv_ref[...] = r[:, NSA_KV_WIDTH:]


def _in_proj(x2, w_pad):
    m = x2.shape[0]
    return pl.pallas_call(
        _in_proj_kernel,
        grid=(m // PROJ_TM,),
        in_specs=[pl.BlockSpec((PROJ_TM, D_MODEL), lambda i: (i, 0)),
                  pl.BlockSpec((D_MODEL, N_IN_PAD), lambda i: (0, 0))],
        out_specs=[pl.BlockSpec((PROJ_TM, N_IN_PAD), lambda i: (i, 0)),
                   pl.BlockSpec((PROJ_TM, NSA_KV_WIDTH), lambda i: (i, 0)),
                   pl.BlockSpec((PROJ_TM, NSA_KV_WIDTH), lambda i: (i, 0))],
        out_shape=[jax.ShapeDtypeStruct((m, N_IN_PAD), BF16),
                   jax.ShapeDtypeStruct((m, NSA_KV_WIDTH), F32),
                   jax.ShapeDtypeStruct((m, NSA_KV_WIDTH), F32)],
        compiler_params=_params("parallel"),
        name="in_proj",
    )(x2, w_pad)


def _compress_kernel(hck_ref, hcv_ref, pek_ref, w1k_ref, w2k_ref, pev_ref, w1v_ref, w2v_ref,
                     kc_ref, vc_ref):
    n_rows = hck_ref.shape[0] // CMP_STRIDE
    half = CMP_BLOCK // 2
    for src, pe_ref, w1_ref, w2_ref, out_ref in ((hck_ref, pek_ref, w1k_ref, w2k_ref, kc_ref),
                                                   (hcv_ref, pev_ref, w1v_ref, w2v_ref, vc_ref)):
        acc_a = jnp.zeros((NSA_KV_HEADS * n_rows, CMP_HIDDEN), F32)
        acc_b = jnp.zeros((NSA_KV_HEADS * n_rows, CMP_HIDDEN), F32)
        for l in range(half):
            xl = src[pl.ds(l, n_rows, stride=CMP_STRIDE), :]
            xs = jnp.concatenate([xl[:, g * HEAD_DIM:(g + 1) * HEAD_DIM]
                                  for g in range(NSA_KV_HEADS)], axis=0)
            xa = (xs + pe_ref[l:l + 1, :]).astype(BF16)
            xb = (xs + pe_ref[half + l:half + l + 1, :]).astype(BF16)
            wa = w1_ref[l * HEAD_DIM:(l + 1) * HEAD_DIM, :].astype(BF16)
            wb = w1_ref[(half + l) * HEAD_DIM:(half + l + 1) * HEAD_DIM, :].astype(BF16)
            acc_a = acc_a + jnp.dot(xa, wa, preferred_element_type=F32)
            acc_b = acc_b + jnp.dot(xb, wb, preferred_element_type=F32)
        w2 = w2_ref[...].astype(BF16)
        for g in range(NSA_KV_HEADS):
            a = acc_a[g * n_rows:(g + 1) * n_rows]
            b = acc_b[g * n_rows:(g + 1) * n_rows]
            hid = a + pltpu.roll(b, n_rows - 1, 0)
            act = jax.nn.gelu(hid).astype(BF16)
            out_ref[g] = jnp.dot(act, w2, preferred_element_type=F32).astype(out_ref.dtype)


def _compress(hck, hcv, pe_k, w1_k, w2_k, pe_v, w1_v, w2_v, batch, seq):
    n_rows = seq // CMP_STRIDE
    full = lambda shape: pl.BlockSpec(shape, lambda b: (0,) * len(shape))
    kv_spec = pl.BlockSpec((seq, NSA_KV_WIDTH), lambda b: (b, 0))
    out_spec = pl.BlockSpec((None, NSA_KV_HEADS, n_rows, HEAD_DIM), lambda b: (b, 0, 0, 0))
    out_sds = jax.ShapeDtypeStruct((batch, NSA_KV_HEADS, n_rows, HEAD_DIM), BF16)
    return pl.pallas_call(
        _compress_kernel,
        grid=(batch,),
        in_specs=[kv_spec, kv_spec,
                  full(pe_k.shape), full(w1_k.shape), full(w2_k.shape),
                  full(pe_v.shape), full(w1_v.shape), full(w2_v.shape)],
        out_specs=[out_spec, out_spec],
        out_shape=[out_sds, out_sds],
        compiler_params=_params("parallel"),
        name="compress",
    )(hck, hcv, pe_k, w1_k, w2_k, pe_v, w1_v, w2_v)


def _online_update(s, m, l, acc, v):
    m_new = jnp.maximum(m, jnp.max(s, axis=-1, keepdims=True))
    p = jnp.exp(s - m_new)
    alpha = jnp.exp(m - m_new)
    l_new = alpha * l + jnp.sum(p, axis=-1, keepdims=True)
    acc_new = alpha * acc + jnp.dot(p.astype(BF16), v, preferred_element_type=F32)
    return m_new, l_new, acc_new


def _qk(q, k):
    return lax.dot_general(q, k, (((1,), (1,)), ((), ())), preferred_element_type=F32)


def _diff_kernel(slopes_ref, q_ref, k_ref, v_ref, lq1_ref, lk1_ref, lq2_ref, lk2_ref, g_ref, o_ref,
                 *, lam_init):
    tq = q_ref.shape[0]
    tk = tq
    head = pl.program_id(1)
    qi = pl.program_id(2)
    slope = slopes_ref[head]
    q = q_ref[...]
    qmaps = (q[:, :HEAD_DIM], q[:, HEAD_DIM:])
    col = lax.broadcasted_iota(jnp.int32, (1, tk), 1)
    row = lax.broadcasted_iota(jnp.int32, (tq, 1), 0)

    def step(j, carry, diagonal):
        ks = pl.multiple_of(j * tk, tk)
        k = k_ref[pl.ds(ks, tk), :]
        v = v_ref[pl.ds(ks, tk), :]
        bias = slope * (col + (j - qi) * tk).astype(F32)
        out = []
        for c in range(2):
            m, l, acc = carry[c]
            s = _qk(qmaps[c], k[:, c * HEAD_DIM:(c + 1) * HEAD_DIM]) + bias
            if diagonal:
                s = jnp.where(col <= row, s, NEG_INF)
            out.append(_online_update(s, m, l, acc, v))
        return tuple(out)

    init = tuple((jnp.full((tq, 1), NEG_INF, F32), jnp.zeros((tq, 1), F32),
                  jnp.zeros((tq, 2 * HEAD_DIM), F32)) for _ in range(2))
    carry = lax.fori_loop(0, qi, lambda j, c: step(j, c, False), init)
    (m0, l0, a0), (m1, l1, a1) = step(qi, carry, True)

    lam = (jnp.exp(jnp.sum(lq1_ref[...] * lk1_ref[...], axis=-1, keepdims=True))
           - jnp.exp(jnp.sum(lq2_ref[...] * lk2_ref[...], axis=-1, keepdims=True)) + lam_init)
    o = a0 / l0 - lam * (a1 / l1)
    o = o * lax.rsqrt(jnp.mean(o * o, axis=-1, keepdims=True) + RMS_EPS) * g_ref[...]
    o_ref[...] = (o * (1.0 - lam_init)).astype(o_ref.dtype)


def _diff_attn(h3, lq1, lk1, lq2, lk2, subln_g, slopes):
    batch, seq, _ = h3.shape
    tq = DIFF_TQ
    kb, vb = OFF_DK // LANES, OFF_DV // LANES
    vec = lambda n: pl.BlockSpec((1, n), lambda b, h, i, s: (0, 0))
    grid_spec = pltpu.PrefetchScalarGridSpec(
        num_scalar_prefetch=1,
        grid=(batch, DIFF_HEADS, seq // tq),
        in_specs=[pl.BlockSpec((None, tq, LANES), lambda b, h, i, s: (b, i, h)),
                  pl.BlockSpec((None, seq, LANES), lambda b, h, i, s: (b, 0, kb + h)),
                  pl.BlockSpec((None, seq, LANES), lambda b, h, i, s: (b, 0, vb + h)),
                  vec(HEAD_DIM), vec(HEAD_DIM), vec(HEAD_DIM), vec(HEAD_DIM), vec(2 * HEAD_DIM)],
        out_specs=pl.BlockSpec((None, tq, LANES), lambda b, h, i, s: (b, i, h)),
    )
    return pl.pallas_call(
        functools.partial(_diff_kernel, lam_init=_lambda_init(0)),
        grid_spec=grid_spec,
        out_shape=jax.ShapeDtypeStruct((batch, seq, DIFF_WIDTH), BF16),
        compiler_params=_params("parallel", "parallel", "arbitrary"),
        name="diff_attn",
    )(slopes, h3, h3, h3, lq1, lk1, lq2, lk2, subln_g)


def _stack_heads(q, g):
    base = g * NSA_GROUP
    return jnp.concatenate([q[:, (base + hh) * HEAD_DIM:(base + hh + 1) * HEAD_DIM]
                            for hh in range(NSA_GROUP)], axis=0)


def _split3(x):
    hi = x.astype(BF16)
    r1 = x - hi.astype(F32)
    mid = r1.astype(BF16)
    lo = (r1 - mid.astype(F32)).astype(BF16)
    return hi, mid, lo


def _cmp_select_kernel(q_ref, kc_ref, vc_ref, ocmp_ref, sel_ref):
    tq = q_ref.shape[0]
    n_rows = kc_ref.shape[1]
    n_sel = sel_ref.shape[1] // 4
    qi = pl.program_id(1)
    q = q_ref[...]
    t_row = qi * tq + lax.broadcasted_iota(jnp.int32, (tq, 1), 0)
    n_col = lax.broadcasted_iota(jnp.int32, (1, n_rows), 1)
    mask_c = (n_col * CMP_STRIDE + (CMP_BLOCK - 1)) <= t_row
    maskf = mask_c.astype(F32)

    s_i = lax.broadcasted_iota(jnp.int32, (n_sel, n_rows), 0)
    n_i = lax.broadcasted_iota(jnp.int32, (n_sel, n_rows), 1)
    ov_t = ((n_i * CMP_STRIDE <= s_i * SEL_BLOCK + (SEL_BLOCK - 1))
            & (n_i * CMP_STRIDE + (CMP_BLOCK - 1) >= s_i * SEL_BLOCK)).astype(BF16)

    blk = lax.broadcasted_iota(jnp.int32, (n_sel, tq), 0)
    t_lane = qi * tq + lax.broadcasted_iota(jnp.int32, (n_sel, tq), 1)
    cur = t_lane // SEL_BLOCK
    forced = (blk == 0) | (blk == cur) | (blk == cur - 1)
    future = blk > cur

    sel_rows = []
    for g in range(NSA_KV_HEADS):
        qg = _stack_heads(q, g)
        s = _qk(qg, kc_ref[g]).reshape(NSA_GROUP, tq, n_rows)
        s = jnp.where(mask_c[None], s, NEG_INF)
        e = jnp.exp(s - jnp.max(s, axis=-1, keepdims=True))
        p = e / jnp.sum(e, axis=-1, keepdims=True) * maskf[None]
        o = jnp.dot(p.reshape(NSA_GROUP * tq, n_rows).astype(BF16), vc_ref[g],
                    preferred_element_type=F32)
        for hh in range(NSA_GROUP):
            hd = (g * NSA_GROUP + hh) * HEAD_DIM
            ocmp_ref[:, hd:hd + HEAD_DIM] = o[hh * tq:(hh + 1) * tq].astype(ocmp_ref.dtype)

        psum = p[0] + p[1] + p[2] + p[3]
        imp_t = jnp.zeros((n_sel, tq), F32)
        for piece in _split3(psum):
            imp_t = imp_t + _qk(ov_t, piece)
        val = jnp.where(forced, jnp.inf, jnp.where(future, -jnp.inf, imp_t))
        rank = jnp.zeros((n_sel, tq), F32)
        for sp in range(n_sel):
            other = val[sp:sp + 1, :]
            beats = (other > val) | ((other == val) & (sp < blk))
            rank = rank + jnp.where(beats, 1.0, 0.0)
        sel_rows.append(jnp.where(rank < float(SEL_TOPN), 1.0, 0.0))
    sel_rows.append(jnp.zeros((2 * n_sel, tq), F32))
    sel_t = jnp.concatenate(sel_rows, axis=0)
    sel_ref[...] = sel_t.T.astype(sel_ref.dtype)


def _cmp_select(h3, kc, vc):
    batch, seq, _ = h3.shape
    tq = CMP_TQ
    n_rows = kc.shape[2]
    n_sel = seq // SEL_BLOCK
    assert 4 * n_sel == LANES
    kv_spec = pl.BlockSpec((None, NSA_KV_HEADS, n_rows, HEAD_DIM), lambda b, i: (b, 0, 0, 0))
    return pl.pallas_call(
        _cmp_select_kernel,
        grid=(batch, seq // tq),
        in_specs=[pl.BlockSpec((None, tq, NSA_WIDTH), lambda b, i: (b, i, OFF_NQ // NSA_WIDTH)),
                  kv_spec, kv_spec],
        out_specs=[pl.BlockSpec((None, tq, NSA_WIDTH), lambda b, i: (b, i, 0)),
                   pl.BlockSpec((None, tq, LANES), lambda b, i: (b, i, 0))],
        out_shape=[jax.ShapeDtypeStruct((batch, seq, NSA_WIDTH), BF16),
                   jax.ShapeDtypeStruct((batch, seq, LANES), BF16)],
        compiler_params=_params("parallel", "parallel"),
        name="cmp_select",
    )(h3, kc, vc)


def _nsa_kernel(slopes_ref, q_ref, sk_ref, sv_ref, wk_ref, wv_ref, gate_ref, sel_ref, ocmp_ref, o_ref):
    tq = q_ref.shape[0]
    tk = tq
    n_sel = sel_ref.shape[1] // 4
    qi = pl.program_id(1)
    q = q_ref[...]
    selm = sel_ref[...]
    exp_row = lax.broadcasted_iota(jnp.int32, (LANES, tk), 0)
    exp_col = lax.broadcasted_iota(jnp.int32, (LANES, tk), 1)
    gates = jax.nn.sigmoid(gate_ref[...].astype(F32))
    col = lax.broadcasted_iota(jnp.int32, (1, tk), 1)
    row = lax.broadcasted_iota(jnp.int32, (tq, 1), 0)
    head_row = lax.broadcasted_iota(jnp.int32, (NSA_GROUP, 1, 1), 0)

    for g in range(NSA_KV_HEADS):
        qg = _stack_heads(q, g)
        slope = jnp.zeros((NSA_GROUP, 1, 1), F32)
        for hh in range(NSA_GROUP):
            slope = jnp.where(head_row == hh, slopes_ref[DIFF_HEADS + g * NSA_GROUP + hh], slope)
        lanes = slice(g * HEAD_DIM, (g + 1) * HEAD_DIM)

        def step(j, carry, k_ref, v_ref, mask):
            m, l, acc = carry
            ks = pl.multiple_of(j * tk, tk)
            k = k_ref[pl.ds(ks, tk), lanes]
            v = v_ref[pl.ds(ks, tk), lanes]
            rel = (col + (j - qi) * tk).astype(F32)
            s = _qk(qg, k).reshape(NSA_GROUP, tq, tk) + slope * rel[None]
            s = jnp.where(mask[None], s, NEG_INF).reshape(NSA_GROUP * tq, tk)
            return _online_update(s, m, l, acc, v)

        def sel_step(j, carry):
            key_blk = g * n_sel + j * (tk // SEL_BLOCK) + exp_col // SEL_BLOCK
            expand = jnp.where(exp_row == key_blk, 1.0, 0.0).astype(BF16)
            picked = jnp.dot(selm, expand, preferred_element_type=F32)
            dist = (row - col) + (qi - j) * tk
            return step(j, carry, sk_ref, sv_ref, (picked > 0.5) & (dist >= 0))

        def win_step(j, carry):
            dist = (row - col) + (qi - j) * tk
            return step(j, carry, wk_ref, wv_ref, (dist >= 0) & (dist < WINDOW))

        init = (jnp.full((NSA_GROUP * tq, 1), NEG_INF, F32), jnp.zeros((NSA_GROUP * tq, 1), F32),
                jnp.zeros((NSA_GROUP * tq, HEAD_DIM), F32))
        _, l_s, a_s = lax.fori_loop(0, qi + 1, sel_step, init)
        _, l_w, a_w = lax.fori_loop(jnp.maximum(qi - WINDOW // tk, 0), qi + 1, win_step, init)
        o_sel = a_s / l_s
        o_win = a_w / l_w
        for hh in range(NSA_GROUP):
            hd = g * NSA_GROUP + hh
            rows = slice(hh * tq, (hh + 1) * tq)
            cols = slice(hd * HEAD_DIM, (hd + 1) * HEAD_DIM)
            o = (gates[:, 3 * hd:3 * hd + 1] * ocmp_ref[:, cols].astype(F32)
                 + gates[:, 3 * hd + 1:3 * hd + 2] * o_sel[rows]
                 + gates[:, 3 * hd + 2:3 * hd + 3] * o_win[rows])
            o_ref[:, cols] = o.astype(o_ref.dtype)


def _nsa_attn(h3, sel, ocmp, slopes):
    batch, seq, _ = h3.shape
    tq = NSA_TQ
    kvspec = lambda off: pl.BlockSpec((None, seq, LANES), lambda b, i, s: (b, 0, off // LANES))
    tile = lambda w, blk: pl.BlockSpec((None, tq, w), lambda b, i, s: (b, i, blk))
    grid_spec = pltpu.PrefetchScalarGridSpec(
        num_scalar_prefetch=1,
        grid=(batch, seq // tq),
        in_specs=[tile(NSA_WIDTH, OFF_NQ // NSA_WIDTH),
                  kvspec(OFF_SK), kvspec(OFF_SV), kvspec(OFF_WK), kvspec(OFF_WV),
                  tile(LANES, OFF_G // LANES), tile(LANES, 0), tile(NSA_WIDTH, 0)],
        out_specs=tile(NSA_WIDTH, 0),
    )
    return pl.pallas_call(
        _nsa_kernel,
        grid_spec=grid_spec,
        out_shape=jax.ShapeDtypeStruct((batch, seq, NSA_WIDTH), BF16),
        compiler_params=_params("parallel", "arbitrary"),
        name="nsa_attn",
    )(slopes, h3, h3, h3, h3, h3, h3, sel, ocmp)


def _layer_norm(y, g, b):
    mu = jnp.mean(y, axis=-1, keepdims=True)
    yc = y - mu
    var = jnp.mean(yc * yc, axis=-1, keepdims=True)
    return yc * lax.rsqrt(var + LN_EPS) * g + b


def _out_ffn_kernel(od_ref, on_ref, x_ref, wo_ref, g1_ref, b1_ref, wg_ref, wu_ref, wd_ref,
                    g2_ref, b2_ref, o_ref):
    mix = (jnp.dot(od_ref[...], wo_ref[:DIFF_WIDTH, :], preferred_element_type=F32)
           + jnp.dot(on_ref[...], wo_ref[DIFF_WIDTH:, :], preferred_element_type=F32))
    x1 = _layer_norm(DEEPNORM_ALPHA * x_ref[...] + mix, g1_ref[...], b1_ref[...])
    x1b = x1.astype(BF16)
    acc = jnp.zeros(x1.shape, F32)
    for c0 in range(0, D_FF, FFN_TF):
        gate = jnp.dot(x1b, wg_ref[:, c0:c0 + FFN_TF], preferred_element_type=F32)
        up = jnp.dot(x1b, wu_ref[:, c0:c0 + FFN_TF], preferred_element_type=F32)
        act = (jax.nn.silu(gate) * up).astype(BF16)
        acc = acc + jnp.dot(act, wd_ref[c0:c0 + FFN_TF, :], preferred_element_type=F32)
    o_ref[...] = _layer_norm(DEEPNORM_ALPHA * x1 + acc, g2_ref[...], b2_ref[...])


def _out_ffn(od, on, x2, wo, g1, b1, wg, wu, wd, g2, b2):
    m = x2.shape[0]
    tm = FFN_TM
    const = lambda shape: pl.BlockSpec(shape, lambda i: (0, 0), pipeline_mode=pl.Buffered(1))
    return pl.pallas_call(
        _out_ffn_kernel,
        grid=(m // tm,),
        in_specs=[pl.BlockSpec((tm, DIFF_WIDTH), lambda i: (i, 0)),
                  pl.BlockSpec((tm, NSA_WIDTH), lambda i: (i, 0)),
                  pl.BlockSpec((tm, D_MODEL), lambda i: (i, 0)),
                  const(wo.shape), const(g1.shape), const(b1.shape),
                  const(wg.shape), const(wu.shape), const(wd.shape),
                  const(g2.shape), const(b2.shape)],
        out_specs=pl.BlockSpec((tm, D_MODEL), lambda i: (i, 0)),
        out_shape=jax.ShapeDtypeStruct((m, D_MODEL), F32),
        compiler_params=_params("parallel"),
        name="out_ffn",
    )(od, on, x2, wo, g1, b1, wg, wu, wd, g2, b2)


def kernel(x, w_in, diff_lq1, diff_lk1, diff_lq2, diff_lk2, diff_subln_g, cmp_pe_k, cmp_w1_k, cmp_w2_k,
           cmp_pe_v, cmp_w1_v, cmp_w2_v, w_out, ln1_g, ln1_b, w_gate, w_up, w_down, ln2_g, ln2_b):
    batch, seq, _ = x.shape
    assert w_in.shape[0] == DEPTH
    assert seq % DIFF_TQ == 0 and seq % CMP_TQ == 0 and seq % NSA_TQ == 0
    assert (batch * seq) % PROJ_TM == 0 and (batch * seq) % FFN_TM == 0
    slopes = jnp.asarray(_alibi_slopes())
    x2 = x.reshape(batch * seq, D_MODEL)

    w_pad = jnp.pad(w_in[0], ((0, 0), (0, N_IN_PAD - N_IN))).astype(BF16)
    h, hck, hcv = _in_proj(x2, w_pad)
    h3 = h.reshape(batch, seq, N_IN_PAD)

    kc, vc = _compress(hck, hcv, cmp_pe_k[0], cmp_w1_k[0], cmp_w2_k[0],
                       cmp_pe_v[0], cmp_w1_v[0], cmp_w2_v[0], batch, seq)
    o_diff = _diff_attn(h3, diff_lq1, diff_lk1, diff_lq2, diff_lk2, diff_subln_g, slopes)
    o_cmp, sel = _cmp_select(h3, kc, vc)
    o_nsa = _nsa_attn(h3, sel, o_cmp, slopes)

    out = _out_ffn(o_diff.reshape(batch * seq, DIFF_WIDTH), o_nsa.reshape(batch * seq, NSA_WIDTH), x2,
                   w_out[0].astype(BF16), ln1_g, ln1_b,
                   w_gate[0].astype(BF16), w_up[0].astype(BF16), w_down[0].astype(BF16), ln2_g, ln2_b)
    return out.reshape(batch, seq, D_MODEL)
```

```python
import functools
import math

import numpy as np
import jax
import jax.numpy as jnp
from jax import lax
from jax.experimental import pallas as pl
from jax.experimental.pallas import tpu as pltpu

F32 = jnp.float32
BF16 = jnp.bfloat16

D_MODEL = 1024
HEAD_DIM = 64
DIFF_HEADS = 4
DIFF_WIDTH = DIFF_HEADS * 2 * HEAD_DIM
NSA_HEADS = 8
NSA_KV_HEADS = 2
NSA_GROUP = NSA_HEADS // NSA_KV_HEADS
NSA_WIDTH = NSA_HEADS * HEAD_DIM
NSA_KV_WIDTH = NSA_KV_HEADS * HEAD_DIM
CMP_BLOCK = 32
CMP_STRIDE = 16
CMP_HIDDEN = 128
SEL_BLOCK = 64
SEL_TOPN = 16
WINDOW = 512
N_GATES = 3 * NSA_HEADS
D_FF = 2816
N_ALIBI = DIFF_HEADS + NSA_HEADS
LN_EPS = 1e-5
RMS_EPS = 1e-5
NEG_INF = -1e30
DEPTH = 1
DEEPNORM_ALPHA = (2.0 * DEPTH) ** 0.25
SCALE = HEAD_DIM ** -0.5
LOG2E = 1.4426950408889634
Q_SCALE = SCALE * LOG2E

OFF_DQ = 0
OFF_DK = OFF_DQ + DIFF_WIDTH
OFF_DV = OFF_DK + DIFF_WIDTH
OFF_NQ = OFF_DV + DIFF_WIDTH
OFF_CK = OFF_NQ + NSA_WIDTH
OFF_CV = OFF_CK + NSA_KV_WIDTH
OFF_SK = OFF_CV + NSA_KV_WIDTH
OFF_SV = OFF_SK + NSA_KV_WIDTH
OFF_WK = OFF_SV + NSA_KV_WIDTH
OFF_WV = OFF_WK + NSA_KV_WIDTH
OFF_G = OFF_WV + NSA_KV_WIDTH
N_IN = OFF_G + N_GATES

LANES = 128
N_IN_PAD = -(-N_IN // LANES) * LANES
VMEM_LIMIT = 56 * 1024 * 1024

PROJ_TM = 512
PROJ_TN = 256
DIFF_TQ = 256
CMP_TQ = 256
NSA_TQ = 128
KV_CHUNK = 512
FFN_TM = 512
FFN_TF = 256

POS_HI_UNIT = 256
N_PIECES = 3
N_BIAS_COLS = 2 * N_PIECES
NSA_EXTRA_BASE = (HEAD_DIM, 0)


def _lambda_init(layer_idx):
    return 0.8 - 0.6 * math.exp(-0.3 * layer_idx)


def _alibi_slopes():
    return np.asarray(2.0 ** (-8.0 * (np.arange(N_ALIBI) + 1) / N_ALIBI), dtype=np.float32)


def _params(*sem):
    return pltpu.CompilerParams(dimension_semantics=sem, vmem_limit_bytes=VMEM_LIMIT)


def _bf16_pieces(x):
    rest = np.asarray(x, np.float32)
    out = []
    for _ in range(N_PIECES):
        piece = rest.astype(BF16).astype(np.float32)
        out.append(piece)
        rest = rest - piece
    return out


def _bias_columns(seq):
    t = np.arange(seq)
    hi = ((t // POS_HI_UNIT) * POS_HI_UNIT).astype(np.float32)
    lo = (t % POS_HI_UNIT).astype(np.float32)
    return np.stack([hi] * N_PIECES + [lo] * N_PIECES, axis=1)


def _slope_columns(slopes):
    pieces = _bf16_pieces(np.asarray(slopes, np.float32) * np.float32(LOG2E))
    return np.stack(pieces + pieces, axis=1)


def _causal_stack(rows, width, period):
    r = (np.arange(rows) % period)[:, None]
    c = np.arange(width)[None, :]
    zero = np.zeros((rows, width), np.float32)
    diag = np.where(c <= r, 0.0, NEG_INF).astype(np.float32)
    full = np.full((rows, width), NEG_INF, np.float32)
    tail = np.where(c > r, 0.0, NEG_INF).astype(np.float32)
    return np.stack([zero, diag, full, tail])


def _qk(q, k):
    return lax.dot_general(q, k, (((1,), (1,)), ((), ())), preferred_element_type=F32)


def _softmax_pv(score_ref, n_tiles, row_max, v):
    mb = jnp.broadcast_to(jnp.max(row_max, axis=-1, keepdims=True), row_max.shape)
    ps = [jnp.exp2(score_ref[:, t * LANES:(t + 1) * LANES] - mb).astype(BF16) for t in range(n_tiles)]
    return jnp.dot(jnp.concatenate(ps, axis=1), v, preferred_element_type=F32)


def _in_proj_kernel(x_ref, w_ref, h_ref, hck_ref, hcv_ref):
    xb = x_ref[...].astype(BF16)
    for c0 in range(0, N_IN_PAD, PROJ_TN):
        c1 = min(c0 + PROJ_TN, N_IN_PAD)
        r = jnp.dot(xb, w_ref[:, c0:c1], preferred_element_type=F32)
        if c0 < OFF_DK or OFF_NQ <= c0 < OFF_CK:
            r = r * Q_SCALE
        h_ref[:, c0:c1] = r.astype(BF16)
        if c0 == OFF_CK:
            hck_ref[...] = r[:, :NSA_KV_WIDTH]
            hcv_ref[...] = r[:, NSA_KV_WIDTH:]


def _in_proj(x2, w_pad):
    m = x2.shape[0]
    return pl.pallas_call(
        _in_proj_kernel,
        grid=(m // PROJ_TM,),
        in_specs=[pl.BlockSpec((PROJ_TM, D_MODEL), lambda i: (i, 0)),
                  pl.BlockSpec((D_MODEL, N_IN_PAD), lambda i: (0, 0))],
        out_specs=[pl.BlockSpec((PROJ_TM, N_IN_PAD), lambda i: (i, 0)),
                   pl.BlockSpec((PROJ_TM, NSA_KV_WIDTH), lambda i: (i, 0)),
                   pl.BlockSpec((PROJ_TM, NSA_KV_WIDTH), lambda i: (i, 0))],
        out_shape=[jax.ShapeDtypeStruct((m, N_IN_PAD), BF16),
                   jax.ShapeDtypeStruct((m, NSA_KV_WIDTH), F32),
                   jax.ShapeDtypeStruct((m, NSA_KV_WIDTH), F32)],
        compiler_params=_params("parallel"),
        name="in_proj",
    )(x2, w_pad)


def _compress_kernel(hck_ref, hcv_ref, pek_ref, w1k_ref, w2k_ref, pev_ref, w1v_ref, w2v_ref,
                     kc_ref, vc_ref):
    n_rows = hck_ref.shape[0] // CMP_STRIDE
    half = CMP_BLOCK // 2
    for src, pe_ref, w1_ref, w2_ref, out_ref in ((hck_ref, pek_ref, w1k_ref, w2k_ref, kc_ref),
                                                   (hcv_ref, pev_ref, w1v_ref, w2v_ref, vc_ref)):
        acc_a = jnp.zeros((NSA_KV_HEADS * n_rows, CMP_HIDDEN), F32)
        acc_b = jnp.zeros((NSA_KV_HEADS * n_rows, CMP_HIDDEN), F32)
        for l in range(half):
            xl = src[pl.ds(l, n_rows, stride=CMP_STRIDE), :]
            xs = jnp.concatenate([xl[:, g * HEAD_DIM:(g + 1) * HEAD_DIM]
                                  for g in range(NSA_KV_HEADS)], axis=0)
            xa = (xs + pe_ref[l:l + 1, :]).astype(BF16)
            xb = (xs + pe_ref[half + l:half + l + 1, :]).astype(BF16)
            wa = w1_ref[l * HEAD_DIM:(l + 1) * HEAD_DIM, :].astype(BF16)
            wb = w1_ref[(half + l) * HEAD_DIM:(half + l + 1) * HEAD_DIM, :].astype(BF16)
            acc_a = acc_a + jnp.dot(xa, wa, preferred_element_type=F32)
            acc_b = acc_b + jnp.dot(xb, wb, preferred_element_type=F32)
        w2 = w2_ref[...].astype(BF16)
        for g in range(NSA_KV_HEADS):
            a = acc_a[g * n_rows:(g + 1) * n_rows]
            b = acc_b[g * n_rows:(g + 1) * n_rows]
            hid = a + pltpu.roll(b, n_rows - 1, 0)
            act = jax.nn.gelu(hid).astype(BF16)
            out_ref[g] = jnp.dot(act, w2, preferred_element_type=F32).astype(out_ref.dtype)


def _compress(hck, hcv, pe_k, w1_k, w2_k, pe_v, w1_v, w2_v, batch, seq):
    n_rows = seq // CMP_STRIDE
    full = lambda shape: pl.BlockSpec(shape, lambda b: (0,) * len(shape))
    kv_spec = pl.BlockSpec((seq, NSA_KV_WIDTH), lambda b: (b, 0))
    out_spec = pl.BlockSpec((None, NSA_KV_HEADS, n_rows, HEAD_DIM), lambda b: (b, 0, 0, 0))
    out_sds = jax.ShapeDtypeStruct((batch, NSA_KV_HEADS, n_rows, HEAD_DIM), BF16)
    return pl.pallas_call(
        _compress_kernel,
        grid=(batch,),
        in_specs=[kv_spec, kv_spec,
                  full(pe_k.shape), full(w1_k.shape), full(w2_k.shape),
                  full(pe_v.shape), full(w1_v.shape), full(w2_v.shape)],
        out_specs=[out_spec, out_spec],
        out_shape=[out_sds, out_sds],
        compiler_params=_params("parallel"),
        name="compress",
    )(hck, hcv, pe_k, w1_k, w2_k, pe_v, w1_v, w2_v)


def _diff_kernel(q_ref, k_ref, v_ref, pos_ref, srow_ref, mask_ref, lq1_ref, lk1_ref, lq2_ref, lk2_ref,
                 g_ref, o_ref, kaug_ref, vaug_ref, s_ref, *, lam_init):
    tq = q_ref.shape[0]
    seq = k_ref.shape[0]
    dv = v_ref.shape[1]
    qi = pl.program_id(2)
    sub = KV_CHUNK // tq
    low = lax.broadcasted_iota(jnp.int32, (1, LANES), 1) < HEAD_DIM
    halves = (low, jnp.logical_not(low))

    @pl.when(qi == 0)
    def _():
        k = k_ref[...]
        for c in range(2):
            kaug_ref[c] = jnp.where(halves[c], k, pos_ref[...])
        vaug_ref[:, :dv] = v_ref[...]
        vaug_ref[:, dv:] = jnp.ones((seq, dv), BF16)

    q = q_ref[...]
    srow = jnp.broadcast_to(srow_ref[...], q.shape)
    q_aug = [jnp.where(halves[c], q, srow) for c in range(2)]
    lam = (jnp.exp(jnp.sum(lq1_ref[...] * lk1_ref[...], axis=-1, keepdims=True))
           - jnp.exp(jnp.sum(lq2_ref[...] * lk2_ref[...], axis=-1, keepdims=True)) + lam_init)

    def sweep(n_chunks):
        diag = qi - sub * (n_chunks - 1)
        outs = []
        for c in range(2):
            mx = jnp.full((tq, LANES), NEG_INF, F32)
            for ch in range(n_chunks):
                s = _qk(q_aug[c], kaug_ref[c, ch * KV_CHUNK:(ch + 1) * KV_CHUNK, :])
                for t in range(sub):
                    tile = s[:, t * tq:(t + 1) * tq]
                    if ch == n_chunks - 1:
                        which = jnp.where(t < diag, 0, jnp.where(t == diag, 1, 2))
                        tile = tile + mask_ref[which]
                    s_ref[c, :, (ch * sub + t) * tq:(ch * sub + t + 1) * tq] = tile
                    for u in range(tq // LANES):
                        mx = jnp.maximum(mx, tile[:, u * LANES:(u + 1) * LANES])
            acc = _softmax_pv(s_ref.at[c], n_chunks * KV_CHUNK // LANES, mx,
                              vaug_ref[0:n_chunks * KV_CHUNK, :])
            outs.append(acc[:, :dv] / acc[:, dv:])
        o = outs[0] - lam * outs[1]
        o = o * lax.rsqrt(jnp.mean(o * o, axis=-1, keepdims=True) + RMS_EPS) * g_ref[...]
        o_ref[...] = (o * (1.0 - lam_init)).astype(o_ref.dtype)

    for n_chunks in range(1, seq // KV_CHUNK + 1):
        pl.when(qi // sub + 1 == n_chunks)(functools.partial(sweep, n_chunks))


def _diff_attn(h3, lq1, lk1, lq2, lk2, subln_g, slopes):
    batch, seq, _ = h3.shape
    tq = DIFF_TQ
    assert KV_CHUNK % tq == 0 and seq % KV_CHUNK == 0
    kb, vb = OFF_DK // LANES, OFF_DV // LANES
    dv = 2 * HEAD_DIM

    bias = _bias_columns(seq)
    pos = np.zeros((seq, LANES), np.float32)
    srow = np.zeros((DIFF_HEADS, 1, LANES), np.float32)
    slope_cols = _slope_columns(slopes[:DIFF_HEADS])
    for base in (0, HEAD_DIM):
        pos[:, base:base + N_BIAS_COLS] = bias
        srow[:, 0, base:base + N_BIAS_COLS] = slope_cols
    masks = _causal_stack(tq, tq, tq)[:3]

    vec = lambda n: pl.BlockSpec((1, n), lambda b, h, i: (0, 0))
    return pl.pallas_call(
        functools.partial(_diff_kernel, lam_init=_lambda_init(0)),
        grid=(batch, DIFF_HEADS, seq // tq),
        in_specs=[pl.BlockSpec((None, tq, LANES), lambda b, h, i: (b, i, h)),
                  pl.BlockSpec((None, seq, LANES), lambda b, h, i: (b, 0, kb + h)),
                  pl.BlockSpec((None, seq, LANES), lambda b, h, i: (b, 0, vb + h)),
                  pl.BlockSpec((seq, LANES), lambda b, h, i: (0, 0)),
                  pl.BlockSpec((None, 1, LANES), lambda b, h, i: (h, 0, 0)),
                  pl.BlockSpec(masks.shape, lambda b, h, i: (0, 0, 0)),
                  vec(HEAD_DIM), vec(HEAD_DIM), vec(HEAD_DIM), vec(HEAD_DIM), vec(dv)],
        out_specs=pl.BlockSpec((None, tq, LANES), lambda b, h, i: (b, i, h)),
        out_shape=jax.ShapeDtypeStruct((batch, seq, DIFF_WIDTH), BF16),
        scratch_shapes=[pltpu.VMEM((2, seq, LANES), BF16),
                        pltpu.VMEM((seq, 2 * dv), BF16),
                        pltpu.VMEM((2, tq, seq), F32)],
        compiler_params=_params("parallel", "parallel", "arbitrary"),
        name="diff_attn",
    )(h3, h3, h3, jnp.asarray(pos, BF16), jnp.asarray(srow, BF16), jnp.asarray(masks),
      lq1, lk1, lq2, lk2, subln_g)


def _head_lanes(g, i):
    start = i * LANES + g * HEAD_DIM
    return slice(start, start + HEAD_DIM)


def _split3(x):
    hi = x.astype(BF16)
    r1 = x - hi.astype(F32)
    mid = r1.astype(BF16)
    lo = (r1 - mid.astype(F32)).astype(BF16)
    return hi, mid, lo


def _cmp_select_kernel(q_ref, kc_ref, vc_ref, ocmp_ref, selx_ref):
    tq = q_ref.shape[0]
    n_rows = kc_ref.shape[1]
    n_sel = LANES // 4
    qi = pl.program_id(1)
    q = q_ref[...]
    t_row = qi * tq + lax.broadcasted_iota(jnp.int32, (tq, 1), 0)
    n_col = lax.broadcasted_iota(jnp.int32, (1, n_rows), 1)
    mask_c = (n_col * CMP_STRIDE + (CMP_BLOCK - 1)) <= t_row
    maskf = mask_c.astype(F32)

    s_i = lax.broadcasted_iota(jnp.int32, (n_sel, n_rows), 0)
    n_i = lax.broadcasted_iota(jnp.int32, (n_sel, n_rows), 1)
    ov_t = ((n_i * CMP_STRIDE <= s_i * SEL_BLOCK + (SEL_BLOCK - 1))
            & (n_i * CMP_STRIDE + (CMP_BLOCK - 1) >= s_i * SEL_BLOCK)).astype(BF16)

    blk = lax.broadcasted_iota(jnp.int32, (n_sel, tq), 0)
    t_lane = qi * tq + lax.broadcasted_iota(jnp.int32, (n_sel, tq), 1)
    cur = t_lane // SEL_BLOCK
    forced = (blk == 0) | (blk == cur) | (blk == cur - 1)
    future = blk > cur

    for g in range(NSA_KV_HEADS):
        qg = jnp.concatenate([q[:, _head_lanes(g, i)] for i in range(NSA_GROUP)], axis=0)
        s = _qk(qg, kc_ref[g]).reshape(NSA_GROUP, tq, n_rows)
        s = jnp.where(mask_c[None], s, NEG_INF)
        e = jnp.exp2(s - jnp.max(s, axis=-1, keepdims=True))
        p = e / jnp.sum(e, axis=-1, keepdims=True) * maskf[None]
        o = jnp.dot(p.reshape(NSA_GROUP * tq, n_rows).astype(BF16), vc_ref[g],
                    preferred_element_type=F32)
        for i in range(NSA_GROUP):
            hd = (g * NSA_GROUP + i) * HEAD_DIM
            ocmp_ref[:, hd:hd + HEAD_DIM] = o[i * tq:(i + 1) * tq].astype(ocmp_ref.dtype)

        psum = p[0] + p[1] + p[2] + p[3]
        imp_t = jnp.zeros((n_sel, tq), F32)
        for piece in _split3(psum):
            imp_t = imp_t + _qk(ov_t, piece)
        val = jnp.where(forced, jnp.inf, jnp.where(future, -jnp.inf, imp_t))
        rank = jnp.zeros((n_sel, tq), F32)
        for sp in range(n_sel):
            other = val[sp:sp + 1, :]
            beats = (other > val) | ((other == val) & (sp < blk))
            rank = rank + jnp.where(beats, 1.0, 0.0)
        pen = jnp.where((rank < float(SEL_TOPN)) & jnp.logical_not(future), 0.0, NEG_INF)
        base = NSA_EXTRA_BASE[g]
        rows = [jnp.zeros((base, tq), F32)] if base else []
        rows += [pen, jnp.zeros((LANES - base - n_sel, tq), F32)]
        selx_ref[:, g * LANES:(g + 1) * LANES] = jnp.concatenate(rows, axis=0).T.astype(selx_ref.dtype)


def _cmp_select(h3, kc, vc):
    batch, seq, _ = h3.shape
    tq = CMP_TQ
    n_rows = kc.shape[2]
    assert 4 * (seq // SEL_BLOCK) == LANES
    kv_spec = pl.BlockSpec((None, NSA_KV_HEADS, n_rows, HEAD_DIM), lambda b, i: (b, 0, 0, 0))
    return pl.pallas_call(
        _cmp_select_kernel,
        grid=(batch, seq // tq),
        in_specs=[pl.BlockSpec((None, tq, NSA_WIDTH), lambda b, i: (b, i, OFF_NQ // NSA_WIDTH)),
                  kv_spec, kv_spec],
        out_specs=[pl.BlockSpec((None, tq, NSA_WIDTH), lambda b, i: (b, i, 0)),
                   pl.BlockSpec((None, tq, NSA_KV_HEADS * LANES), lambda b, i: (b, i, 0))],
        out_shape=[jax.ShapeDtypeStruct((batch, seq, NSA_WIDTH), BF16),
                   jax.ShapeDtypeStruct((batch, seq, NSA_KV_HEADS * LANES), BF16)],
        compiler_params=_params("parallel", "parallel"),
        name="cmp_select",
    )(h3, kc, vc)


def _nsa_kernel(q_ref, sk_ref, sv_ref, wk_ref, wv_ref, gate_ref, selx_ref, ocmp_ref, ext_ref, srow_ref,
                mask_ref, o_ref, ksel_ref, vsel_ref, kwin_ref, vwin_ref, ssel_ref, swin_ref):
    tq = q_ref.shape[0]
    seq = sk_ref.shape[0]
    rows = NSA_GROUP * tq
    qi = pl.program_id(1)
    sub = KV_CHUNK // tq
    win_tiles = WINDOW // tq + 1
    low = lax.broadcasted_iota(jnp.int32, (1, LANES), 1) < HEAD_DIM
    halves = (low, jnp.logical_not(low))

    @pl.when(qi == 0)
    def _():
        for g in range(NSA_KV_HEADS):
            ksel_ref[g] = jnp.where(halves[g], sk_ref[...], ext_ref[g])
            kwin_ref[g] = jnp.where(halves[g], wk_ref[...], ext_ref[g])
            vsel_ref[g] = jnp.where(halves[g], sv_ref[...], jnp.ones((seq, LANES), BF16))
            vwin_ref[g] = jnp.where(halves[g], wv_ref[...], jnp.ones((seq, LANES), BF16))

    q = q_ref[...]
    gates = jax.nn.sigmoid(gate_ref[...].astype(F32))

    def sweep(n_chunks):
        diag = qi - sub * (n_chunks - 1)
        for g in range(NSA_KV_HEADS):
            pen = selx_ref[:, g * LANES:(g + 1) * LANES]
            q_sel, q_win = [], []
            for i in range(NSA_GROUP):
                blk = q[:, i * LANES:(i + 1) * LANES]
                srow = jnp.broadcast_to(srow_ref[g * NSA_GROUP + i], blk.shape)
                q_sel.append(jnp.where(halves[g], blk, pen + srow))
                q_win.append(jnp.where(halves[g], blk, srow))
            q_sel = jnp.concatenate(q_sel, axis=0)
            q_win = jnp.concatenate(q_win, axis=0)

            mx = jnp.full((rows, LANES), NEG_INF, F32)
            for ch in range(n_chunks):
                s = _qk(q_sel, ksel_ref[g, ch * KV_CHUNK:(ch + 1) * KV_CHUNK, :])
                for t in range(sub):
                    tile = s[:, t * tq:(t + 1) * tq]
                    if ch == n_chunks - 1:
                        which = jnp.where(t < diag, 0, jnp.where(t == diag, 1, 2))
                        tile = tile + mask_ref[which]
                    ssel_ref[g, :, (ch * sub + t) * tq:(ch * sub + t + 1) * tq] = tile
                    mx = jnp.maximum(mx, tile)
            acc_s = _softmax_pv(ssel_ref.at[g], n_chunks * sub, mx, vsel_ref[g, 0:n_chunks * KV_CHUNK, :])

            start = pl.multiple_of(jnp.maximum(qi - (win_tiles - 1), 0) * tq, tq)
            s = _qk(q_win, kwin_ref[g, pl.ds(start, win_tiles * tq), :])
            mx = jnp.full((rows, LANES), NEG_INF, F32)
            for t in range(win_tiles):
                tile = s[:, t * tq:(t + 1) * tq]
                if n_chunks == 1:
                    which = jnp.where(t < qi, 0, jnp.where(t == qi, 1, 2))
                    tile = tile + mask_ref[which]
                elif t == 0:
                    tile = tile + mask_ref[3]
                elif t == win_tiles - 1:
                    tile = tile + mask_ref[1]
                swin_ref[g, :, t * tq:(t + 1) * tq] = tile
                mx = jnp.maximum(mx, tile)
            acc_w = _softmax_pv(swin_ref.at[g], win_tiles, mx, vwin_ref[g, pl.ds(start, win_tiles * tq), :])

            o_sel = acc_s / pltpu.roll(acc_s, HEAD_DIM, 1)
            o_win = acc_w / pltpu.roll(acc_w, HEAD_DIM, 1)
            lanes = slice(g * HEAD_DIM, (g + 1) * HEAD_DIM)
            for i in range(NSA_GROUP):
                hd = g * NSA_GROUP + i
                rsl = slice(i * tq, (i + 1) * tq)
                cols = slice(hd * HEAD_DIM, (hd + 1) * HEAD_DIM)
                o = (gates[:, 3 * hd:3 * hd + 1] * ocmp_ref[:, cols].astype(F32)
                     + gates[:, 3 * hd + 1:3 * hd + 2] * o_sel[rsl, lanes]
                     + gates[:, 3 * hd + 2:3 * hd + 3] * o_win[rsl, lanes])
                o_ref[:, cols] = o.astype(o_ref.dtype)

    for n_chunks in range(1, seq // KV_CHUNK + 1):
        pl.when(qi // sub + 1 == n_chunks)(functools.partial(sweep, n_chunks))


def _nsa_attn(h3, selx, ocmp, slopes):
    batch, seq, _ = h3.shape
    tq = NSA_TQ
    assert KV_CHUNK % tq == 0 and seq % KV_CHUNK == 0 and WINDOW % tq == 0 and WINDOW <= KV_CHUNK
    n_sel = seq // SEL_BLOCK
    rows = NSA_GROUP * tq
    win_keys = WINDOW + tq

    bias = _bias_columns(seq)
    onehot = (np.arange(seq)[:, None] // SEL_BLOCK == np.arange(n_sel)[None, :]).astype(np.float32)
    ext = np.zeros((NSA_KV_HEADS, seq, LANES), np.float32)
    srow = np.zeros((NSA_HEADS, 1, LANES), np.float32)
    slope_cols = _slope_columns(slopes[DIFF_HEADS:])
    for g in range(NSA_KV_HEADS):
        base = NSA_EXTRA_BASE[g]
        ext[g, :, base:base + n_sel] = onehot
        ext[g, :, base + n_sel:base + n_sel + N_BIAS_COLS] = bias
        srow[g * NSA_GROUP:(g + 1) * NSA_GROUP, 0, base + n_sel:base + n_sel + N_BIAS_COLS] = \
            slope_cols[g * NSA_GROUP:(g + 1) * NSA_GROUP]
    masks = _causal_stack(rows, tq, tq)

    kvspec = lambda off: pl.BlockSpec((None, seq, LANES), lambda b, i: (b, 0, off // LANES))
    tile = lambda w, blk: pl.BlockSpec((None, tq, w), lambda b, i: (b, i, blk))
    const = lambda a: pl.BlockSpec(a.shape, lambda b, i: (0,) * a.ndim)
    return pl.pallas_call(
        _nsa_kernel,
        grid=(batch, seq // tq),
        in_specs=[tile(NSA_WIDTH, OFF_NQ // NSA_WIDTH),
                  kvspec(OFF_SK), kvspec(OFF_SV), kvspec(OFF_WK), kvspec(OFF_WV),
                  tile(LANES, OFF_G // LANES), tile(NSA_KV_HEADS * LANES, 0), tile(NSA_WIDTH, 0),
                  const(ext), const(srow), const(masks)],
        out_specs=tile(NSA_WIDTH, 0),
        out_shape=jax.ShapeDtypeStruct((batch, seq, NSA_WIDTH), BF16),
        scratch_shapes=[pltpu.VMEM((NSA_KV_HEADS, seq, LANES), BF16) for _ in range(4)]
                       + [pltpu.VMEM((NSA_KV_HEADS, rows, seq), F32),
                          pltpu.VMEM((NSA_KV_HEADS, rows, win_keys), F32)],
        compiler_params=_params("parallel", "arbitrary"),
        name="nsa_attn",
    )(h3, h3, h3, h3, h3, h3, selx, ocmp, jnp.asarray(ext, BF16), jnp.asarray(srow, BF16),
      jnp.asarray(masks))


def _layer_norm(y, g, b):
    mu = jnp.mean(y, axis=-1, keepdims=True)
    yc = y - mu
    var = jnp.mean(yc * yc, axis=-1, keepdims=True)
    return yc * lax.rsqrt(var + LN_EPS) * g + b


def _out_ffn_kernel(od_ref, on_ref, x_ref, wo_ref, g1_ref, b1_ref, wg_ref, wu_ref, wd_ref,
                    g2_ref, b2_ref, o_ref):
    mix = (jnp.dot(od_ref[...], wo_ref[:DIFF_WIDTH, :], preferred_element_type=F32)
           + jnp.dot(on_ref[...], wo_ref[DIFF_WIDTH:, :], preferred_element_type=F32))
    x1 = _layer_norm(DEEPNORM_ALPHA * x_ref[...] + mix, g1_ref[...], b1_ref[...])
    x1b = x1.astype(BF16)
    acc = jnp.zeros(x1.shape, F32)
    for c0 in range(0, D_FF, FFN_TF):
        gate = jnp.dot(x1b, wg_ref[:, c0:c0 + FFN_TF], preferred_element_type=F32)
        up = jnp.dot(x1b, wu_ref[:, c0:c0 + FFN_TF], preferred_element_type=F32)
        act = (jax.nn.silu(gate) * up).astype(BF16)
        acc = acc + jnp.dot(act, wd_ref[c0:c0 + FFN_TF, :], preferred_element_type=F32)
    o_ref[...] = _layer_norm(DEEPNORM_ALPHA * x1 + acc, g2_ref[...], b2_ref[...])


def _out_ffn(od, on, x2, wo, g1, b1, wg, wu, wd, g2, b2):
    m = x2.shape[0]
    tm = FFN_TM
    const = lambda shape: pl.BlockSpec(shape, lambda i: (0, 0), pipeline_mode=pl.Buffered(1))
    return pl.pallas_call(
        _out_ffn_kernel,
        grid=(m // tm,),
        in_specs=[pl.BlockSpec((tm, DIFF_WIDTH), lambda i: (i, 0)),
                  pl.BlockSpec((tm, NSA_WIDTH), lambda i: (i, 0)),
                  pl.BlockSpec((tm, D_MODEL), lambda i: (i, 0)),
                  const(wo.shape), const(g1.shape), const(b1.shape),
                  const(wg.shape), const(wu.shape), const(wd.shape),
                  const(g2.shape), const(b2.shape)],
        out_specs=pl.BlockSpec((tm, D_MODEL), lambda i: (i, 0)),
        out_shape=jax.ShapeDtypeStruct((m, D_MODEL), F32),
        compiler_params=_params("parallel"),
        name="out_ffn",
    )(od, on, x2, wo, g1, b1, wg, wu, wd, g2, b2)


def kernel(x, w_in, diff_lq1, diff_lk1, diff_lq2, diff_lk2, diff_subln_g, cmp_pe_k, cmp_w1_k, cmp_w2_k,
           cmp_pe_v, cmp_w1_v, cmp_w2_v, w_out, ln1_g, ln1_b, w_gate, w_up, w_down, ln2_g, ln2_b):
    batch, seq, _ = x.shape
    assert w_in.shape[0] == DEPTH
    assert seq % DIFF_TQ == 0 and seq % CMP_TQ == 0 and seq % NSA_TQ == 0
    assert (batch * seq) % PROJ_TM == 0 and (batch * seq) % FFN_TM == 0
    slopes = _alibi_slopes()
    x2 = x.reshape(batch * seq, D_MODEL)

    w = w_in[0]
    head_order = [g * NSA_GROUP + i for i in range(NSA_GROUP) for g in range(NSA_KV_HEADS)]
    w_nq = w[:, OFF_NQ:OFF_CK].reshape(D_MODEL, NSA_HEADS, HEAD_DIM)[:, head_order, :]
    w_pad = jnp.concatenate([w[:, :OFF_NQ], w_nq.reshape(D_MODEL, NSA_WIDTH), w[:, OFF_CK:],
                             jnp.zeros((D_MODEL, N_IN_PAD - N_IN), w.dtype)], axis=1).astype(BF16)
    h, hck, hcv = _in_proj(x2, w_pad)
    h3 = h.reshape(batch, seq, N_IN_PAD)

    kc, vc = _compress(hck, hcv, cmp_pe_k[0], cmp_w1_k[0], cmp_w2_k[0],
                       cmp_pe_v[0], cmp_w1_v[0], cmp_w2_v[0], batch, seq)
    o_diff = _diff_attn(h3, diff_lq1, diff_lk1, diff_lq2, diff_lk2, diff_subln_g, slopes)
    o_cmp, selx = _cmp_select(h3, kc, vc)
    o_nsa = _nsa_attn(h3, selx, o_cmp, slopes)

    out = _out_ffn(o_diff.reshape(batch * seq, DIFF_WIDTH), o_nsa.reshape(batch * seq, NSA_WIDTH), x2,
                   w_out[0].astype(BF16), ln1_g, ln1_b,
                   w_gate[0].astype(BF16), w_up[0].astype(BF16), w_down[0].astype(BF16), ln2_g, ln2_b)
    return out.reshape(batch, seq, D_MODEL)
```

```python
import functools
import math

import numpy as np
import jax
import jax.numpy as jnp
from jax import lax
from jax.experimental import pallas as pl
from jax.experimental.pallas import tpu as pltpu

F32 = jnp.float32
BF16 = jnp.bfloat16

D_MODEL = 1024
HEAD_DIM = 64
DIFF_HEADS = 4
DIFF_WIDTH = DIFF_HEADS * 2 * HEAD_DIM
NSA_HEADS = 8
NSA_KV_HEADS = 2
NSA_GROUP = NSA_HEADS // NSA_KV_HEADS
NSA_WIDTH = NSA_HEADS * HEAD_DIM
NSA_KV_WIDTH = NSA_KV_HEADS * HEAD_DIM
CMP_BLOCK = 32
CMP_STRIDE = 16
CMP_HIDDEN = 128
SEL_BLOCK = 64
SEL_TOPN = 16
WINDOW = 512
N_GATES = 3 * NSA_HEADS
D_FF = 2816
N_ALIBI = DIFF_HEADS + NSA_HEADS
LN_EPS = 1e-5
RMS_EPS = 1e-5
NEG_INF = -1e30
DEPTH = 1
DEEPNORM_ALPHA = (2.0 * DEPTH) ** 0.25
SCALE = HEAD_DIM ** -0.5
LOG2E = 1.4426950408889634
Q_SCALE = SCALE * LOG2E

OFF_DQ = 0
OFF_DK = OFF_DQ + DIFF_WIDTH
OFF_DV = OFF_DK + DIFF_WIDTH
OFF_NQ = OFF_DV + DIFF_WIDTH
OFF_CK = OFF_NQ + NSA_WIDTH
OFF_CV = OFF_CK + NSA_KV_WIDTH
OFF_SK = OFF_CV + NSA_KV_WIDTH
OFF_SV = OFF_SK + NSA_KV_WIDTH
OFF_WK = OFF_SV + NSA_KV_WIDTH
OFF_WV = OFF_WK + NSA_KV_WIDTH
OFF_G = OFF_WV + NSA_KV_WIDTH
N_IN = OFF_G + N_GATES

LANES = 128
N_IN_PAD = -(-N_IN // LANES) * LANES
VMEM_LIMIT = 56 * 1024 * 1024

PROJ_TM = 512
PROJ_TN = 256
CMP_TQ = 256
DIFF_STEP = 2048
NSA_STEP = 1024
ATT_TQ = 128
SCORE_BUFS = 4
KV_CHUNK = 512
FFN_TM = 512
FFN_TF = 256

POS_HI_UNIT = 256
N_PIECES = 3
N_BIAS_COLS = 2 * N_PIECES
NSA_EXTRA_BASE = (HEAD_DIM, 0)


def _lambda_init(layer_idx):
    return 0.8 - 0.6 * math.exp(-0.3 * layer_idx)


def _alibi_slopes():
    return np.asarray(2.0 ** (-8.0 * (np.arange(N_ALIBI) + 1) / N_ALIBI), dtype=np.float32)


def _params(*sem):
    return pltpu.CompilerParams(dimension_semantics=sem, vmem_limit_bytes=VMEM_LIMIT)


def _bf16_pieces(x):
    rest = np.asarray(x, np.float32)
    out = []
    for _ in range(N_PIECES):
        piece = rest.astype(BF16).astype(np.float32)
        out.append(piece)
        rest = rest - piece
    return out


def _bias_columns(seq):
    t = np.arange(seq)
    hi = ((t // POS_HI_UNIT) * POS_HI_UNIT).astype(np.float32)
    lo = (t % POS_HI_UNIT).astype(np.float32)
    return np.stack([hi] * N_PIECES + [lo] * N_PIECES, axis=1)


def _slope_columns(slopes):
    pieces = _bf16_pieces(np.asarray(slopes, np.float32) * np.float32(LOG2E))
    return np.stack(pieces + pieces, axis=1)


def _causal_stack(rows, width, period):
    r = (np.arange(rows) % period)[:, None]
    c = np.arange(width)[None, :]
    zero = np.zeros((rows, width), np.float32)
    diag = np.where(c <= r, 0.0, NEG_INF).astype(np.float32)
    full = np.full((rows, width), NEG_INF, np.float32)
    tail = np.where(c > r, 0.0, NEG_INF).astype(np.float32)
    return np.stack([zero, diag, full, tail])


def _qk(q, k):
    return lax.dot_general(q, k, (((1,), (1,)), ((), ())), preferred_element_type=F32)


def _attend(q_aug, k_ref, v_ref, score_ref, key_start, n_tiles, masks):
    rows = q_aug.shape[0]
    per_chunk = KV_CHUNK // LANES
    mx = jnp.full((rows, LANES), NEG_INF, F32)
    for c0 in range(0, n_tiles, per_chunk):
        c1 = min(c0 + per_chunk, n_tiles)
        s = _qk(q_aug, k_ref[key_start + c0 * LANES:key_start + c1 * LANES, :])
        for t in range(c0, c1):
            tile = s[:, (t - c0) * LANES:(t - c0 + 1) * LANES]
            if t in masks:
                tile = tile + masks[t]
            score_ref[:, t * LANES:(t + 1) * LANES] = tile
            mx = jnp.maximum(mx, tile)
    mb = jnp.broadcast_to(jnp.max(mx, axis=-1, keepdims=True), mx.shape)
    ps = [jnp.exp2(score_ref[:, t * LANES:(t + 1) * LANES] - mb).astype(BF16) for t in range(n_tiles)]
    v = v_ref[key_start:key_start + n_tiles * LANES, :]
    return jnp.dot(jnp.concatenate(ps, axis=1), v, preferred_element_type=F32)


def _in_proj_kernel(x_ref, w_ref, h_ref, hck_ref, hcv_ref):
    xb = x_ref[...].astype(BF16)
    for c0 in range(0, N_IN_PAD, PROJ_TN):
        c1 = min(c0 + PROJ_TN, N_IN_PAD)
        r = jnp.dot(xb, w_ref[:, c0:c1], preferred_element_type=F32)
        if c0 < OFF_DK or OFF_NQ <= c0 < OFF_CK:
            r = r * Q_SCALE
        h_ref[:, c0:c1] = r.astype(BF16)
        if c0 == OFF_CK:
            hck_ref[...] = r[:, :NSA_KV_WIDTH]
            hcv_ref[...] = r[:, NSA_KV_WIDTH:]


def _in_proj(x2, w_pad):
    m = x2.shape[0]
    return pl.pallas_call(
        _in_proj_kernel,
        grid=(m // PROJ_TM,),
        in_specs=[pl.BlockSpec((PROJ_TM, D_MODEL), lambda i: (i, 0)),
                  pl.BlockSpec((D_MODEL, N_IN_PAD), lambda i: (0, 0))],
        out_specs=[pl.BlockSpec((PROJ_TM, N_IN_PAD), lambda i: (i, 0)),
                   pl.BlockSpec((PROJ_TM, NSA_KV_WIDTH), lambda i: (i, 0)),
                   pl.BlockSpec((PROJ_TM, NSA_KV_WIDTH), lambda i: (i, 0))],
        out_shape=[jax.ShapeDtypeStruct((m, N_IN_PAD), BF16),
                   jax.ShapeDtypeStruct((m, NSA_KV_WIDTH), F32),
                   jax.ShapeDtypeStruct((m, NSA_KV_WIDTH), F32)],
        compiler_params=_params("parallel"),
        name="in_proj",
    )(x2, w_pad)


def _compress_kernel(hck_ref, hcv_ref, pek_ref, w1k_ref, w2k_ref, pev_ref, w1v_ref, w2v_ref,
                     kc_ref, vc_ref):
    n_rows = hck_ref.shape[0] // CMP_STRIDE
    half = CMP_BLOCK // 2
    for src, pe_ref, w1_ref, w2_ref, out_ref in ((hck_ref, pek_ref, w1k_ref, w2k_ref, kc_ref),
                                                   (hcv_ref, pev_ref, w1v_ref, w2v_ref, vc_ref)):
        acc_a = jnp.zeros((NSA_KV_HEADS * n_rows, CMP_HIDDEN), F32)
        acc_b = jnp.zeros((NSA_KV_HEADS * n_rows, CMP_HIDDEN), F32)
        for l in range(half):
            xl = src[pl.ds(l, n_rows, stride=CMP_STRIDE), :]
            xs = jnp.concatenate([xl[:, g * HEAD_DIM:(g + 1) * HEAD_DIM]
                                  for g in range(NSA_KV_HEADS)], axis=0)
            xa = (xs + pe_ref[l:l + 1, :]).astype(BF16)
            xb = (xs + pe_ref[half + l:half + l + 1, :]).astype(BF16)
            wa = w1_ref[l * HEAD_DIM:(l + 1) * HEAD_DIM, :].astype(BF16)
            wb = w1_ref[(half + l) * HEAD_DIM:(half + l + 1) * HEAD_DIM, :].astype(BF16)
            acc_a = acc_a + jnp.dot(xa, wa, preferred_element_type=F32)
            acc_b = acc_b + jnp.dot(xb, wb, preferred_element_type=F32)
        w2 = w2_ref[...].astype(BF16)
        for g in range(NSA_KV_HEADS):
            a = acc_a[g * n_rows:(g + 1) * n_rows]
            b = acc_b[g * n_rows:(g + 1) * n_rows]
            hid = a + pltpu.roll(b, n_rows - 1, 0)
            act = jax.nn.gelu(hid).astype(BF16)
            if out_ref is kc_ref:
                out = jnp.dot(act, w2, preferred_element_type=F32)
            else:
                out = _qk(w2, act)
            out_ref[g] = out.astype(out_ref.dtype)


def _compress(hck, hcv, pe_k, w1_k, w2_k, pe_v, w1_v, w2_v, batch, seq):
    n_rows = seq // CMP_STRIDE
    w2_vt = w2_v.T
    full = lambda shape: pl.BlockSpec(shape, lambda b: (0,) * len(shape))
    kv_spec = pl.BlockSpec((seq, NSA_KV_WIDTH), lambda b: (b, 0))
    out_spec = lambda r, c: pl.BlockSpec((None, NSA_KV_HEADS, r, c), lambda b: (b, 0, 0, 0))
    out_sds = lambda r, c: jax.ShapeDtypeStruct((batch, NSA_KV_HEADS, r, c), BF16)
    return pl.pallas_call(
        _compress_kernel,
        grid=(batch,),
        in_specs=[kv_spec, kv_spec,
                  full(pe_k.shape), full(w1_k.shape), full(w2_k.shape),
                  full(pe_v.shape), full(w1_v.shape), full(w2_vt.shape)],
        out_specs=[out_spec(n_rows, HEAD_DIM), out_spec(HEAD_DIM, n_rows)],
        out_shape=[out_sds(n_rows, HEAD_DIM), out_sds(HEAD_DIM, n_rows)],
        compiler_params=_params("parallel"),
        name="compress",
    )(hck, hcv, pe_k, w1_k, w2_k, pe_v, w1_v, w2_vt)


def _diff_kernel(q_ref, k_ref, v_ref, pos_ref, srow_ref, mask_ref, lq1_ref, lk1_ref, lq2_ref, lk2_ref,
                 g_ref, o_ref, kaug_ref, vaug_ref, s_ref, *, lam_init):
    step = q_ref.shape[0]
    seq = k_ref.shape[0]
    dv = v_ref.shape[1]
    tq = ATT_TQ
    qi = pl.program_id(2)
    low = lax.broadcasted_iota(jnp.int32, (1, LANES), 1) < HEAD_DIM
    halves = (low, jnp.logical_not(low))

    @pl.when(qi == 0)
    def _():
        k = k_ref[...]
        for c in range(2):
            kaug_ref[c] = jnp.where(halves[c], k, pos_ref[...])
        vaug_ref[:, :dv] = v_ref[...]
        vaug_ref[:, dv:] = jnp.ones((seq, dv), BF16)

    q = q_ref[...]
    srow = jnp.broadcast_to(srow_ref[...], q.shape)
    q_aug = [jnp.where(halves[c], q, srow) for c in range(2)]
    lam = (jnp.exp(jnp.sum(lq1_ref[...] * lk1_ref[...], axis=-1, keepdims=True))
           - jnp.exp(jnp.sum(lq2_ref[...] * lk2_ref[...], axis=-1, keepdims=True)) + lam_init)

    def sweep(step_idx):
        for r in range(step // tq):
            rows = slice(r * tq, (r + 1) * tq)
            n_tiles = (step_idx * step + (r + 1) * tq) // LANES
            outs = []
            for c in range(2):
                acc = _attend(q_aug[c][rows], kaug_ref.at[c], vaug_ref, s_ref.at[c, r % SCORE_BUFS], 0, n_tiles,
                              {n_tiles - 1: mask_ref[...]})
                outs.append(acc[:, :dv] / acc[:, dv:])
            o = outs[0] - lam * outs[1]
            o = o * lax.rsqrt(jnp.mean(o * o, axis=-1, keepdims=True) + RMS_EPS) * g_ref[...]
            o_ref[rows, :] = (o * (1.0 - lam_init)).astype(o_ref.dtype)

    for step_idx in range(seq // step):
        pl.when(qi == step_idx)(functools.partial(sweep, step_idx))


def _diff_attn(h3, lq1, lk1, lq2, lk2, subln_g, slopes):
    batch, seq, _ = h3.shape
    tq = DIFF_STEP
    assert seq % tq == 0 and tq % ATT_TQ == 0 and ATT_TQ == LANES
    kb, vb = OFF_DK // LANES, OFF_DV // LANES
    dv = 2 * HEAD_DIM

    bias = _bias_columns(seq)
    pos = np.zeros((seq, LANES), np.float32)
    srow = np.zeros((DIFF_HEADS, 1, LANES), np.float32)
    slope_cols = _slope_columns(slopes[:DIFF_HEADS])
    for base in (0, HEAD_DIM):
        pos[:, base:base + N_BIAS_COLS] = bias
        srow[:, 0, base:base + N_BIAS_COLS] = slope_cols
    masks = _causal_stack(ATT_TQ, LANES, ATT_TQ)[1]

    vec = lambda n: pl.BlockSpec((1, n), lambda b, h, i: (0, 0))
    return pl.pallas_call(
        functools.partial(_diff_kernel, lam_init=_lambda_init(0)),
        grid=(batch, DIFF_HEADS, seq // tq),
        in_specs=[pl.BlockSpec((None, tq, LANES), lambda b, h, i: (b, i, h)),
                  pl.BlockSpec((None, seq, LANES), lambda b, h, i: (b, 0, kb + h)),
                  pl.BlockSpec((None, seq, LANES), lambda b, h, i: (b, 0, vb + h)),
                  pl.BlockSpec((seq, LANES), lambda b, h, i: (0, 0)),
                  pl.BlockSpec((None, 1, LANES), lambda b, h, i: (h, 0, 0)),
                  pl.BlockSpec(masks.shape, lambda b, h, i: (0, 0)),
                  vec(HEAD_DIM), vec(HEAD_DIM), vec(HEAD_DIM), vec(HEAD_DIM), vec(dv)],
        out_specs=pl.BlockSpec((None, tq, LANES), lambda b, h, i: (b, i, h)),
        out_shape=jax.ShapeDtypeStruct((batch, seq, DIFF_WIDTH), BF16),
        scratch_shapes=[pltpu.VMEM((2, seq, LANES), BF16),
                        pltpu.VMEM((seq, 2 * dv), BF16),
                        pltpu.VMEM((2, SCORE_BUFS, ATT_TQ, seq), F32)],
        compiler_params=_params("parallel", "parallel", "arbitrary"),
        name="diff_attn",
    )(h3, h3, h3, jnp.asarray(pos, BF16), jnp.asarray(srow, BF16), jnp.asarray(masks),
      lq1, lk1, lq2, lk2, subln_g)


def _head_lanes(g, i):
    start = i * LANES + g * HEAD_DIM
    return slice(start, start + HEAD_DIM)


def _split3(x):
    hi = x.astype(BF16)
    r1 = x - hi.astype(F32)
    mid = r1.astype(BF16)
    lo = (r1 - mid.astype(F32)).astype(BF16)
    return hi, mid, lo


def _cmp_select_kernel(q_ref, kc_ref, vct_ref, ocmp_ref, selx_ref):
    tq = q_ref.shape[0]
    n_rows = kc_ref.shape[1]
    n_sel = LANES // 4
    qi = pl.program_id(1)
    q = q_ref[...]
    slot = lax.broadcasted_iota(jnp.int32, (n_rows, tq), 0)
    t_q = qi * tq + lax.broadcasted_iota(jnp.int32, (n_rows, tq), 1)
    mask_c = (slot * CMP_STRIDE + (CMP_BLOCK - 1)) <= t_q
    maskf = jnp.where(mask_c, 1.0, 0.0)

    s_i = lax.broadcasted_iota(jnp.int32, (n_sel, n_rows), 0)
    n_i = lax.broadcasted_iota(jnp.int32, (n_sel, n_rows), 1)
    ov_t = ((n_i * CMP_STRIDE <= s_i * SEL_BLOCK + (SEL_BLOCK - 1))
            & (n_i * CMP_STRIDE + (CMP_BLOCK - 1) >= s_i * SEL_BLOCK)).astype(BF16)

    blk = lax.broadcasted_iota(jnp.int32, (n_sel, tq), 0)
    t_lane = qi * tq + lax.broadcasted_iota(jnp.int32, (n_sel, tq), 1)
    cur = t_lane // SEL_BLOCK
    forced = (blk == 0) | (blk == cur) | (blk == cur - 1)
    future = blk > cur

    for g in range(NSA_KV_HEADS):
        qg = jnp.concatenate([q[:, _head_lanes(g, i)] for i in range(NSA_GROUP)], axis=0)
        s_t = _qk(kc_ref[g], qg)
        ps = []
        for i in range(NSA_GROUP):
            s = jnp.where(mask_c, s_t[:, i * tq:(i + 1) * tq], NEG_INF)
            e = jnp.exp2(s - jnp.max(s, axis=0, keepdims=True))
            ps.append(e / jnp.sum(e, axis=0, keepdims=True) * maskf)
        o_t = jnp.dot(vct_ref[g], jnp.concatenate(ps, axis=1).astype(BF16),
                      preferred_element_type=F32)
        for j in range(NSA_GROUP // 2):
            pair = jnp.concatenate([o_t[:, (2 * j) * tq:(2 * j + 1) * tq],
                                    o_t[:, (2 * j + 1) * tq:(2 * j + 2) * tq]], axis=0)
            ocmp_ref[g * (NSA_GROUP // 2) + j] = pair.T.astype(ocmp_ref.dtype)

        psum = ps[0] + ps[1] + ps[2] + ps[3]
        imp_t = jnp.zeros((n_sel, tq), F32)
        for piece in _split3(psum):
            imp_t = imp_t + jnp.dot(ov_t, piece, preferred_element_type=F32)
        val = jnp.where(forced, jnp.inf, jnp.where(future, -jnp.inf, imp_t))
        rank = jnp.zeros((n_sel, tq), F32)
        for sp in range(n_sel):
            other = val[sp:sp + 1, :]
            beats = (other > val) | ((other == val) & (sp < blk))
            rank = rank + jnp.where(beats, 1.0, 0.0)
        pen = jnp.where((rank < float(SEL_TOPN)) & jnp.logical_not(future), 0.0, NEG_INF)
        base = NSA_EXTRA_BASE[g]
        rows = [jnp.zeros((base, tq), F32)] if base else []
        rows += [pen, jnp.zeros((LANES - base - n_sel, tq), F32)]
        selx_ref[g] = jnp.concatenate(rows, axis=0).T.astype(selx_ref.dtype)


def _cmp_select(h3, kc, vct):
    batch, seq, _ = h3.shape
    tq = CMP_TQ
    n_rows = kc.shape[2]
    n_pairs = NSA_HEADS // 2
    assert 4 * (seq // SEL_BLOCK) == LANES and 2 * HEAD_DIM == LANES
    kv_spec = lambda a: pl.BlockSpec((None,) + a.shape[1:], lambda b, i: (b, 0, 0, 0))
    return pl.pallas_call(
        _cmp_select_kernel,
        grid=(batch, seq // tq),
        in_specs=[pl.BlockSpec((None, tq, NSA_WIDTH), lambda b, i: (b, i, OFF_NQ // NSA_WIDTH)),
                  kv_spec(kc), kv_spec(vct)],
        out_specs=[pl.BlockSpec((None, n_pairs, tq, LANES), lambda b, i: (b, 0, i, 0)),
                   pl.BlockSpec((None, NSA_KV_HEADS, tq, LANES), lambda b, i: (b, 0, i, 0))],
        out_shape=[jax.ShapeDtypeStruct((batch, n_pairs, seq, LANES), BF16),
                   jax.ShapeDtypeStruct((batch, NSA_KV_HEADS, seq, LANES), BF16)],
        compiler_params=_params("parallel", "parallel"),
        name="cmp_select",
    )(h3, kc, vct)


def _nsa_kernel(q_ref, sk_ref, sv_ref, wk_ref, wv_ref, gate_ref, selx_ref, ocmp_ref, ext_ref, srow_ref,
                mask_ref, gexp_ref, o_ref, ksel_ref, vsel_ref, kwin_ref, vwin_ref, ssel_ref, swin_ref):
    step = q_ref.shape[0]
    seq = sk_ref.shape[0]
    tq = ATT_TQ
    qi = pl.program_id(1)
    win_tiles = WINDOW // tq + 1
    lane = lax.broadcasted_iota(jnp.int32, (1, LANES), 1)
    halves = (lane < HEAD_DIM, lane >= HEAD_DIM)

    @pl.when(qi == 0)
    def _():
        for g in range(NSA_KV_HEADS):
            ksel_ref[g] = jnp.where(halves[g], sk_ref[...], ext_ref[g])
            kwin_ref[g] = jnp.where(halves[g], wk_ref[...], ext_ref[g])
            vsel_ref[g] = jnp.where(halves[g], sv_ref[...], jnp.ones((seq, LANES), BF16))
            vwin_ref[g] = jnp.where(halves[g], wv_ref[...], jnp.ones((seq, LANES), BF16))

    def group_sweep(step_idx, g, carry):
        half = (lane // HEAD_DIM) == g
        is_g0 = g == 0
        q = q_ref[...]
        pen = selx_ref[g]
        diag, tail = mask_ref[0], mask_ref[1]
        gsig = jax.nn.sigmoid(gate_ref[...].astype(F32)).astype(BF16)
        for r in range(step // tq):
            rsl = slice(r * tq, (r + 1) * tq)
            q_sel, q_win = [], []
            for i in range(NSA_GROUP):
                blk = q[rsl, i * LANES:(i + 1) * LANES]
                srow = jnp.broadcast_to(srow_ref[g * NSA_GROUP + i], blk.shape)
                q_sel.append(jnp.where(half, blk, pen[rsl] + srow))
                q_win.append(jnp.where(half, blk, srow))
            q_sel = jnp.concatenate(q_sel, axis=0)
            q_win = jnp.concatenate(q_win, axis=0)

            n_tiles = (step_idx * step + (r + 1) * tq) // LANES
            acc_s = _attend(q_sel, ksel_ref.at[g], vsel_ref.at[g], ssel_ref.at[r % SCORE_BUFS], 0, n_tiles,
                            {n_tiles - 1: diag})
            first = n_tiles - win_tiles
            if first < 0:
                acc_w = _attend(q_win, kwin_ref.at[g], vwin_ref.at[g], swin_ref.at[r % SCORE_BUFS], 0, n_tiles,
                                {n_tiles - 1: diag})
            else:
                acc_w = _attend(q_win, kwin_ref.at[g], vwin_ref.at[g], swin_ref.at[r % SCORE_BUFS], first * LANES,
                                win_tiles, {0: tail, win_tiles - 1: diag})
            o_sel = acc_s / pltpu.roll(acc_s, HEAD_DIM, 1)
            o_win = acc_w / pltpu.roll(acc_w, HEAD_DIM, 1)
            gate = [jnp.dot(gsig[rsl], gexp_ref[g, br], preferred_element_type=F32) for br in range(3)]
            for j in range(NSA_GROUP // 2):
                first = slice((2 * j) * tq, (2 * j + 1) * tq)
                second = slice((2 * j + 1) * tq, (2 * j + 2) * tq)

                def head_pair(o):
                    keep = jnp.where(is_g0, o[first], o[second])
                    move = jnp.where(is_g0, o[second], o[first])
                    return jnp.where(half, keep, pltpu.roll(move, HEAD_DIM, 1))

                cols = slice(j * LANES, (j + 1) * LANES)
                blk_idx = g * (NSA_GROUP // 2) + j
                out = (gate[0][:, cols] * ocmp_ref[blk_idx, rsl, :].astype(F32)
                       + gate[1][:, cols] * head_pair(o_sel) + gate[2][:, cols] * head_pair(o_win))
                o_ref[blk_idx, rsl, :] = out.astype(o_ref.dtype)
        return carry

    for step_idx in range(seq // step):
        @pl.when(qi == step_idx)
        def _(step_idx=step_idx):
            lax.fori_loop(0, NSA_KV_HEADS, functools.partial(group_sweep, step_idx), 0)


def _nsa_attn(h3, selx, ocmp, slopes):
    batch, seq, _ = h3.shape
    tq = NSA_STEP
    assert seq % tq == 0 and tq % ATT_TQ == 0 and ATT_TQ == LANES and WINDOW % ATT_TQ == 0
    n_sel = seq // SEL_BLOCK
    rows = NSA_GROUP * ATT_TQ
    win_keys = WINDOW + ATT_TQ

    bias = _bias_columns(seq)
    onehot = (np.arange(seq)[:, None] // SEL_BLOCK == np.arange(n_sel)[None, :]).astype(np.float32)
    ext = np.zeros((NSA_KV_HEADS, seq, LANES), np.float32)
    srow = np.zeros((NSA_HEADS, 1, LANES), np.float32)
    slope_cols = _slope_columns(slopes[DIFF_HEADS:])
    for g in range(NSA_KV_HEADS):
        base = NSA_EXTRA_BASE[g]
        ext[g, :, base:base + n_sel] = onehot
        ext[g, :, base + n_sel:base + n_sel + N_BIAS_COLS] = bias
        srow[g * NSA_GROUP:(g + 1) * NSA_GROUP, 0, base + n_sel:base + n_sel + N_BIAS_COLS] = \
            slope_cols[g * NSA_GROUP:(g + 1) * NSA_GROUP]
    masks = _causal_stack(rows, LANES, ATT_TQ)[[1, 3]]
    gexp = np.zeros((NSA_KV_HEADS, 3, LANES, NSA_GROUP * HEAD_DIM), np.float32)
    for g in range(NSA_KV_HEADS):
        for br in range(3):
            for i in range(NSA_GROUP):
                gexp[g, br, 3 * (g * NSA_GROUP + i) + br, i * HEAD_DIM:(i + 1) * HEAD_DIM] = 1.0

    n_pairs = NSA_HEADS // 2
    kvspec = lambda off: pl.BlockSpec((None, seq, LANES), lambda b, i: (b, 0, off // LANES))
    tile = lambda w, blk: pl.BlockSpec((None, tq, w), lambda b, i: (b, i, blk))
    blocks = lambda n: pl.BlockSpec((None, n, tq, LANES), lambda b, i: (b, 0, i, 0))
    const = lambda a: pl.BlockSpec(a.shape, lambda b, i: (0,) * a.ndim)
    return pl.pallas_call(
        _nsa_kernel,
        grid=(batch, seq // tq),
        in_specs=[tile(NSA_WIDTH, OFF_NQ // NSA_WIDTH),
                  kvspec(OFF_SK), kvspec(OFF_SV), kvspec(OFF_WK), kvspec(OFF_WV),
                  tile(LANES, OFF_G // LANES), blocks(NSA_KV_HEADS), blocks(n_pairs),
                  const(ext), const(srow), const(masks), const(gexp)],
        out_specs=blocks(n_pairs),
        out_shape=jax.ShapeDtypeStruct((batch, n_pairs, seq, LANES), BF16),
        scratch_shapes=[pltpu.VMEM((NSA_KV_HEADS, seq, LANES), BF16) for _ in range(4)]
                       + [pltpu.VMEM((SCORE_BUFS, rows, seq), F32),
                          pltpu.VMEM((SCORE_BUFS, rows, win_keys), F32)],
        compiler_params=_params("parallel", "arbitrary"),
        name="nsa_attn",
    )(h3, h3, h3, h3, h3, h3, selx, ocmp, jnp.asarray(ext, BF16), jnp.asarray(srow, BF16),
      jnp.asarray(masks), jnp.asarray(gexp, BF16))


def _layer_norm(y, g, b):
    mu = jnp.mean(y, axis=-1, keepdims=True)
    yc = y - mu
    var = jnp.mean(yc * yc, axis=-1, keepdims=True)
    return yc * lax.rsqrt(var + LN_EPS) * g + b


def _out_ffn_kernel(od_ref, on_ref, x_ref, wo_ref, g1_ref, b1_ref, wg_ref, wu_ref, wd_ref,
                    g2_ref, b2_ref, o_ref):
    mix = jnp.dot(od_ref[...], wo_ref[:DIFF_WIDTH, :], preferred_element_type=F32)
    for j in range(on_ref.shape[0]):
        rows = slice(DIFF_WIDTH + j * LANES, DIFF_WIDTH + (j + 1) * LANES)
        mix = mix + jnp.dot(on_ref[j], wo_ref[rows, :], preferred_element_type=F32)
    x1 = _layer_norm(DEEPNORM_ALPHA * x_ref[...] + mix, g1_ref[...], b1_ref[...])
    x1b = x1.astype(BF16)
    acc = jnp.zeros(x1.shape, F32)
    for c0 in range(0, D_FF, FFN_TF):
        gate = jnp.dot(x1b, wg_ref[:, c0:c0 + FFN_TF], preferred_element_type=F32)
        up = jnp.dot(x1b, wu_ref[:, c0:c0 + FFN_TF], preferred_element_type=F32)
        act = (jax.nn.silu(gate) * up).astype(BF16)
        acc = acc + jnp.dot(act, wd_ref[c0:c0 + FFN_TF, :], preferred_element_type=F32)
    o_ref[...] = _layer_norm(DEEPNORM_ALPHA * x1 + acc, g2_ref[...], b2_ref[...])


def _out_ffn(od, on, x2, wo, g1, b1, wg, wu, wd, g2, b2):
    m = x2.shape[0]
    tm = FFN_TM
    _, n_pairs, seq, _ = on.shape
    per_seq = seq // tm
    const = lambda shape: pl.BlockSpec(shape, lambda i: (0, 0), pipeline_mode=pl.Buffered(1))
    return pl.pallas_call(
        _out_ffn_kernel,
        grid=(m // tm,),
        in_specs=[pl.BlockSpec((tm, DIFF_WIDTH), lambda i: (i, 0)),
                  pl.BlockSpec((None, n_pairs, tm, LANES), lambda i: (i // per_seq, 0, i % per_seq, 0)),
                  pl.BlockSpec((tm, D_MODEL), lambda i: (i, 0)),
                  const(wo.shape), const(g1.shape), const(b1.shape),
                  const(wg.shape), const(wu.shape), const(wd.shape),
                  const(g2.shape), const(b2.shape)],
        out_specs=pl.BlockSpec((tm, D_MODEL), lambda i: (i, 0)),
        out_shape=jax.ShapeDtypeStruct((m, D_MODEL), F32),
        compiler_params=_params("parallel"),
        name="out_ffn",
    )(od, on, x2, wo, g1, b1, wg, wu, wd, g2, b2)


def kernel(x, w_in, diff_lq1, diff_lk1, diff_lq2, diff_lk2, diff_subln_g, cmp_pe_k, cmp_w1_k, cmp_w2_k,
           cmp_pe_v, cmp_w1_v, cmp_w2_v, w_out, ln1_g, ln1_b, w_gate, w_up, w_down, ln2_g, ln2_b):
    batch, seq, _ = x.shape
    assert w_in.shape[0] == DEPTH
    assert seq % DIFF_STEP == 0 and seq % NSA_STEP == 0 and seq % CMP_TQ == 0
    assert (batch * seq) % PROJ_TM == 0 and (batch * seq) % FFN_TM == 0
    slopes = _alibi_slopes()
    x2 = x.reshape(batch * seq, D_MODEL)

    w = w_in[0]
    head_order = [g * NSA_GROUP + i for i in range(NSA_GROUP) for g in range(NSA_KV_HEADS)]
    w_nq = w[:, OFF_NQ:OFF_CK].reshape(D_MODEL, NSA_HEADS, HEAD_DIM)[:, head_order, :]
    w_pad = jnp.concatenate([w[:, :OFF_NQ], w_nq.reshape(D_MODEL, NSA_WIDTH), w[:, OFF_CK:],
                             jnp.zeros((D_MODEL, N_IN_PAD - N_IN), w.dtype)], axis=1).astype(BF16)
    h, hck, hcv = _in_proj(x2, w_pad)
    h3 = h.reshape(batch, seq, N_IN_PAD)

    kc, vc = _compress(hck, hcv, cmp_pe_k[0], cmp_w1_k[0], cmp_w2_k[0],
                       cmp_pe_v[0], cmp_w1_v[0], cmp_w2_v[0], batch, seq)
    o_diff = _diff_attn(h3, diff_lq1, diff_lk1, diff_lq2, diff_lk2, diff_subln_g, slopes)
    o_cmp, selx = _cmp_select(h3, kc, vc)
    o_nsa = _nsa_attn(h3, selx, o_cmp, slopes)

    assert seq % FFN_TM == 0
    out = _out_ffn(o_diff.reshape(batch * seq, DIFF_WIDTH), o_nsa, x2,
                   w_out[0].astype(BF16), ln1_g, ln1_b,
                   w_gate[0].astype(BF16), w_up[0].astype(BF16), w_down[0].astype(BF16), ln2_g, ln2_b)
    return out.reshape(batch, seq, D_MODEL)
```

```python
import functools
import math

import numpy as np
import jax
import jax.numpy as jnp
from jax import lax
from jax.experimental import pallas as pl
from jax.experimental.pallas import tpu as pltpu

F32 = jnp.float32
BF16 = jnp.bfloat16

D_MODEL = 1024
HEAD_DIM = 64
DIFF_HEADS = 4
DIFF_WIDTH = DIFF_HEADS * 2 * HEAD_DIM
NSA_HEADS = 8
NSA_KV_HEADS = 2
NSA_GROUP = NSA_HEADS // NSA_KV_HEADS
NSA_WIDTH = NSA_HEADS * HEAD_DIM
NSA_KV_WIDTH = NSA_KV_HEADS * HEAD_DIM
CMP_BLOCK = 32
CMP_STRIDE = 16
CMP_HIDDEN = 128
SEL_BLOCK = 64
SEL_TOPN = 16
WINDOW = 512
N_GATES = 3 * NSA_HEADS
D_FF = 2816
N_ALIBI = DIFF_HEADS + NSA_HEADS
LN_EPS = 1e-5
RMS_EPS = 1e-5
NEG_INF = -1e30
DEPTH = 1
DEEPNORM_ALPHA = (2.0 * DEPTH) ** 0.25
SCALE = HEAD_DIM ** -0.5
LOG2E = 1.4426950408889634
Q_SCALE = SCALE * LOG2E

OFF_DQ = 0
OFF_DK = OFF_DQ + DIFF_WIDTH
OFF_DV = OFF_DK + DIFF_WIDTH
OFF_NQ = OFF_DV + DIFF_WIDTH
OFF_CK = OFF_NQ + NSA_WIDTH
OFF_CV = OFF_CK + NSA_KV_WIDTH
OFF_SK = OFF_CV + NSA_KV_WIDTH
OFF_SV = OFF_SK + NSA_KV_WIDTH
OFF_WK = OFF_SV + NSA_KV_WIDTH
OFF_WV = OFF_WK + NSA_KV_WIDTH
OFF_G = OFF_WV + NSA_KV_WIDTH
N_IN = OFF_G + N_GATES

LANES = 128
N_IN_PAD = -(-N_IN // LANES) * LANES
VMEM_LIMIT = 56 * 1024 * 1024

PROJ_TM = 512
PROJ_TN = 256
CMP_TQ = 256
DIFF_STEP = 2048
NSA_STEP = 1024
ATT_TQ = 128
DIFF_TQ = 128
DIFF_HEADS_PER_STEP = 2
SCORE_BUFS = 4
KV_CHUNK = 512
FFN_TM = 1024
FFN_CHAIN = 512
FFN_TF = 256

POS_HI_UNIT = 256
N_PIECES = 3
N_BIAS_COLS = 2 * N_PIECES
NSA_EXTRA_BASE = (HEAD_DIM, 0)


def _lambda_init(layer_idx):
    return 0.8 - 0.6 * math.exp(-0.3 * layer_idx)


def _alibi_slopes():
    return np.asarray(2.0 ** (-8.0 * (np.arange(N_ALIBI) + 1) / N_ALIBI), dtype=np.float32)


def _params(*sem):
    return pltpu.CompilerParams(dimension_semantics=sem, vmem_limit_bytes=VMEM_LIMIT)


def _bf16_pieces(x):
    rest = np.asarray(x, np.float32)
    out = []
    for _ in range(N_PIECES):
        piece = rest.astype(BF16).astype(np.float32)
        out.append(piece)
        rest = rest - piece
    return out


def _bias_columns(seq):
    t = np.arange(seq)
    hi = ((t // POS_HI_UNIT) * POS_HI_UNIT).astype(np.float32)
    lo = (t % POS_HI_UNIT).astype(np.float32)
    return np.stack([hi] * N_PIECES + [lo] * N_PIECES, axis=1)


def _slope_columns(slopes):
    pieces = _bf16_pieces(np.asarray(slopes, np.float32) * np.float32(LOG2E))
    return np.stack(pieces + pieces, axis=1)


def _causal_stack(rows, width, period):
    r = (np.arange(rows) % period)[:, None]
    c = np.arange(width)[None, :]
    zero = np.zeros((rows, width), np.float32)
    diag = np.where(c <= r, 0.0, NEG_INF).astype(np.float32)
    full = np.full((rows, width), NEG_INF, np.float32)
    tail = np.where(c > r, 0.0, NEG_INF).astype(np.float32)
    return np.stack([zero, diag, full, tail])


def _qk(q, k):
    return lax.dot_general(q, k, (((1,), (1,)), ((), ())), preferred_element_type=F32)


def _stage_scores(q_aug, k_ref, score_ref, key_start, n_tiles, masks):
    rows = q_aug.shape[0]
    per_chunk = KV_CHUNK // LANES
    mx = jnp.full((rows, LANES), NEG_INF, F32)
    for c0 in range(0, n_tiles, per_chunk):
        c1 = min(c0 + per_chunk, n_tiles)
        s = _qk(q_aug, k_ref[key_start + c0 * LANES:key_start + c1 * LANES, :])
        for t in range(c0, c1):
            tile = s[:, (t - c0) * LANES:(t - c0 + 1) * LANES]
            if t in masks:
                tile = tile + masks[t]
            score_ref[:, t * LANES:(t + 1) * LANES] = tile
            mx = jnp.maximum(mx, tile)
    return mx


def _softmax_pv(score_ref, lane_max, v_ref, key_start, n_tiles):
    mb = jnp.broadcast_to(jnp.max(lane_max, axis=-1, keepdims=True), lane_max.shape)
    ps = [jnp.exp2(score_ref[:, t * LANES:(t + 1) * LANES] - mb).astype(BF16) for t in range(n_tiles)]
    v = v_ref[key_start:key_start + n_tiles * LANES, :]
    return jnp.dot(jnp.concatenate(ps, axis=1), v, preferred_element_type=F32)


def _in_proj_kernel(x_ref, w_ref, h_ref, hck_ref, hcv_ref):
    xb = x_ref[...].astype(BF16)
    for c0 in range(0, N_IN_PAD, PROJ_TN):
        c1 = min(c0 + PROJ_TN, N_IN_PAD)
        r = jnp.dot(xb, w_ref[:, c0:c1], preferred_element_type=F32)
        if c0 < OFF_DK or OFF_NQ <= c0 < OFF_CK:
            r = r * Q_SCALE
        h_ref[:, c0:c1] = r.astype(BF16)
        if c0 == OFF_CK:
            hck_ref[...] = r[:, :NSA_KV_WIDTH]
            hcv_ref[...] = r[:, NSA_KV_WIDTH:]


def _in_proj(x2, w_pad):
    m = x2.shape[0]
    return pl.pallas_call(
        _in_proj_kernel,
        grid=(m // PROJ_TM,),
        in_specs=[pl.BlockSpec((PROJ_TM, D_MODEL), lambda i: (i, 0)),
                  pl.BlockSpec((D_MODEL, N_IN_PAD), lambda i: (0, 0))],
        out_specs=[pl.BlockSpec((PROJ_TM, N_IN_PAD), lambda i: (i, 0)),
                   pl.BlockSpec((PROJ_TM, NSA_KV_WIDTH), lambda i: (i, 0)),
                   pl.BlockSpec((PROJ_TM, NSA_KV_WIDTH), lambda i: (i, 0))],
        out_shape=[jax.ShapeDtypeStruct((m, N_IN_PAD), BF16),
                   jax.ShapeDtypeStruct((m, NSA_KV_WIDTH), F32),
                   jax.ShapeDtypeStruct((m, NSA_KV_WIDTH), F32)],
        compiler_params=_params("parallel"),
        name="in_proj",
    )(x2, w_pad)


def _compress_kernel(hck_ref, hcv_ref, pek_ref, w1k_ref, w2k_ref, pev_ref, w1v_ref, w2v_ref,
                     kc_ref, vc_ref):
    n_rows = hck_ref.shape[0] // CMP_STRIDE
    half = CMP_BLOCK // 2
    for src, pe_ref, w1_ref, w2_ref, out_ref in ((hck_ref, pek_ref, w1k_ref, w2k_ref, kc_ref),
                                                   (hcv_ref, pev_ref, w1v_ref, w2v_ref, vc_ref)):
        acc_a = jnp.zeros((NSA_KV_HEADS * n_rows, CMP_HIDDEN), F32)
        acc_b = jnp.zeros((NSA_KV_HEADS * n_rows, CMP_HIDDEN), F32)
        for l in range(half):
            xl = src[pl.ds(l, n_rows, stride=CMP_STRIDE), :]
            xs = jnp.concatenate([xl[:, g * HEAD_DIM:(g + 1) * HEAD_DIM]
                                  for g in range(NSA_KV_HEADS)], axis=0)
            xa = (xs + pe_ref[l:l + 1, :]).astype(BF16)
            xb = (xs + pe_ref[half + l:half + l + 1, :]).astype(BF16)
            wa = w1_ref[l * HEAD_DIM:(l + 1) * HEAD_DIM, :].astype(BF16)
            wb = w1_ref[(half + l) * HEAD_DIM:(half + l + 1) * HEAD_DIM, :].astype(BF16)
            acc_a = acc_a + jnp.dot(xa, wa, preferred_element_type=F32)
            acc_b = acc_b + jnp.dot(xb, wb, preferred_element_type=F32)
        w2 = w2_ref[...].astype(BF16)
        for g in range(NSA_KV_HEADS):
            a = acc_a[g * n_rows:(g + 1) * n_rows]
            b = acc_b[g * n_rows:(g + 1) * n_rows]
            hid = a + pltpu.roll(b, n_rows - 1, 0)
            act = jax.nn.gelu(hid).astype(BF16)
            if out_ref is kc_ref:
                out = jnp.dot(act, w2, preferred_element_type=F32)
            else:
                out = _qk(w2, act)
            out_ref[g] = out.astype(out_ref.dtype)


def _compress(hck, hcv, pe_k, w1_k, w2_k, pe_v, w1_v, w2_v, batch, seq):
    n_rows = seq // CMP_STRIDE
    w2_vt = w2_v.T
    full = lambda shape: pl.BlockSpec(shape, lambda b: (0,) * len(shape))
    kv_spec = pl.BlockSpec((seq, NSA_KV_WIDTH), lambda b: (b, 0))
    out_spec = lambda r, c: pl.BlockSpec((None, NSA_KV_HEADS, r, c), lambda b: (b, 0, 0, 0))
    out_sds = lambda r, c: jax.ShapeDtypeStruct((batch, NSA_KV_HEADS, r, c), BF16)
    return pl.pallas_call(
        _compress_kernel,
        grid=(batch,),
        in_specs=[kv_spec, kv_spec,
                  full(pe_k.shape), full(w1_k.shape), full(w2_k.shape),
                  full(pe_v.shape), full(w1_v.shape), full(w2_vt.shape)],
        out_specs=[out_spec(n_rows, HEAD_DIM), out_spec(HEAD_DIM, n_rows)],
        out_shape=[out_sds(n_rows, HEAD_DIM), out_sds(HEAD_DIM, n_rows)],
        compiler_params=_params("parallel"),
        name="compress",
    )(hck, hcv, pe_k, w1_k, w2_k, pe_v, w1_v, w2_vt)


def _diff_kernel(q_ref, k_ref, v_ref, pos_ref, srow_ref, mask_ref, lq1_ref, lk1_ref, lq2_ref, lk2_ref,
                 g_ref, o_ref, kaug_ref, vaug_ref, s_ref, *, lam_init):
    step = q_ref.shape[0]
    seq = k_ref.shape[0]
    dv = 2 * HEAD_DIM
    n_heads = q_ref.shape[1] // LANES
    qi = pl.program_id(2)
    low = lax.broadcasted_iota(jnp.int32, (1, LANES), 1) < HEAD_DIM
    halves = (low, jnp.logical_not(low))
    head_lanes = [slice(hh * LANES, (hh + 1) * LANES) for hh in range(n_heads)]

    @pl.when(qi == 0)
    def _():
        for hh in range(n_heads):
            k = k_ref[:, head_lanes[hh]]
            for c in range(2):
                kaug_ref[hh, c] = jnp.where(halves[c], k, pos_ref[...])
            vaug_ref[hh, :, :dv] = v_ref[:, head_lanes[hh]]
            vaug_ref[hh, :, dv:] = jnp.ones((seq, dv), BF16)

    q_aug = []
    for hh in range(n_heads):
        q = q_ref[:, head_lanes[hh]]
        srow = jnp.broadcast_to(srow_ref[hh:hh + 1, :], q.shape)
        q_aug.append([jnp.where(halves[c], q, srow) for c in range(2)])
    lam = (jnp.exp(jnp.sum(lq1_ref[...] * lk1_ref[...], axis=-1, keepdims=True))
           - jnp.exp(jnp.sum(lq2_ref[...] * lk2_ref[...], axis=-1, keepdims=True)) + lam_init)

    tq = mask_ref.shape[1]
    per_chain = tq // LANES

    def sweep(step_idx):
        for r in range(step // tq):
            rows = slice(r * tq, (r + 1) * tq)
            n_tiles = (step_idx * step + (r + 1) * tq) // LANES
            masks = {n_tiles - per_chain + u: mask_ref[u] for u in range(per_chain)}
            for hh in range(n_heads):
                outs = []
                for c in range(2):
                    buf = s_ref.at[hh * 2 + c, r % SCORE_BUFS]
                    lane_max = _stage_scores(q_aug[hh][c][rows], kaug_ref.at[hh, c], buf, 0, n_tiles, masks)
                    acc = _softmax_pv(buf, lane_max, vaug_ref.at[hh], 0, n_tiles)
                    outs.append(acc[:, :dv] / acc[:, dv:])
                o = outs[0] - lam * outs[1]
                o = o * lax.rsqrt(jnp.mean(o * o, axis=-1, keepdims=True) + RMS_EPS) * g_ref[...]
                o_ref[rows, head_lanes[hh]] = (o * (1.0 - lam_init)).astype(o_ref.dtype)

    for step_idx in range(seq // step):
        pl.when(qi == step_idx)(functools.partial(sweep, step_idx))


def _diff_attn(h3, lq1, lk1, lq2, lk2, subln_g, slopes):
    batch, seq, _ = h3.shape
    tq = DIFF_STEP
    assert seq % tq == 0 and tq % DIFF_TQ == 0 and DIFF_TQ % LANES == 0
    assert DIFF_HEADS % DIFF_HEADS_PER_STEP == 0 and (OFF_DK // LANES) % DIFF_HEADS_PER_STEP == 0
    kb, vb = OFF_DK // LANES, OFF_DV // LANES
    dv = 2 * HEAD_DIM

    bias = _bias_columns(seq)
    pos = np.zeros((seq, LANES), np.float32)
    srow = np.zeros((DIFF_HEADS, 1, LANES), np.float32)
    slope_cols = _slope_columns(slopes[:DIFF_HEADS])
    for base in (0, HEAD_DIM):
        pos[:, base:base + N_BIAS_COLS] = bias
        srow[:, 0, base:base + N_BIAS_COLS] = slope_cols
    masks = _causal_stack(DIFF_TQ, DIFF_TQ, DIFF_TQ)[1].reshape(DIFF_TQ, DIFF_TQ // LANES, LANES).transpose(1, 0, 2)

    hn = DIFF_HEADS_PER_STEP
    width = hn * LANES
    kb, vb = kb // hn, vb // hn
    vec = lambda n: pl.BlockSpec((1, n), lambda b, h, i: (0, 0))
    return pl.pallas_call(
        functools.partial(_diff_kernel, lam_init=_lambda_init(0)),
        grid=(batch, DIFF_HEADS // hn, seq // tq),
        in_specs=[pl.BlockSpec((None, tq, width), lambda b, h, i: (b, i, h)),
                  pl.BlockSpec((None, seq, width), lambda b, h, i: (b, 0, kb + h)),
                  pl.BlockSpec((None, seq, width), lambda b, h, i: (b, 0, vb + h)),
                  pl.BlockSpec((seq, LANES), lambda b, h, i: (0, 0)),
                  pl.BlockSpec((None, hn, LANES), lambda b, h, i: (h, 0, 0)),
                  pl.BlockSpec(masks.shape, lambda b, h, i: (0, 0, 0)),
                  vec(HEAD_DIM), vec(HEAD_DIM), vec(HEAD_DIM), vec(HEAD_DIM), vec(dv)],
        out_specs=pl.BlockSpec((None, tq, width), lambda b, h, i: (b, i, h)),
        out_shape=jax.ShapeDtypeStruct((batch, seq, DIFF_WIDTH), BF16),
        scratch_shapes=[pltpu.VMEM((hn, 2, seq, LANES), BF16),
                        pltpu.VMEM((hn, seq, 2 * dv), BF16),
                        pltpu.VMEM((hn * 2, SCORE_BUFS, DIFF_TQ, seq), F32)],
        compiler_params=_params("parallel", "parallel", "arbitrary"),
        name="diff_attn",
    )(h3, h3, h3, jnp.asarray(pos, BF16), jnp.asarray(srow.reshape(DIFF_HEADS // hn, hn, LANES), BF16),
      jnp.asarray(masks),
      lq1, lk1, lq2, lk2, subln_g)


def _head_lanes(g, i):
    start = i * LANES + g * HEAD_DIM
    return slice(start, start + HEAD_DIM)


def _split3(x):
    hi = x.astype(BF16)
    r1 = x - hi.astype(F32)
    mid = r1.astype(BF16)
    lo = (r1 - mid.astype(F32)).astype(BF16)
    return hi, mid, lo


def _cmp_select_kernel(q_ref, kc_ref, vct_ref, ocmp_ref, selx_ref):
    tq = q_ref.shape[0]
    n_rows = kc_ref.shape[1]
    n_sel = LANES // 4
    qi = pl.program_id(1)
    q = q_ref[...]
    slot = lax.broadcasted_iota(jnp.int32, (n_rows, tq), 0)
    t_q = qi * tq + lax.broadcasted_iota(jnp.int32, (n_rows, tq), 1)
    mask_c = (slot * CMP_STRIDE + (CMP_BLOCK - 1)) <= t_q
    maskf = jnp.where(mask_c, 1.0, 0.0)

    s_i = lax.broadcasted_iota(jnp.int32, (n_sel, n_rows), 0)
    n_i = lax.broadcasted_iota(jnp.int32, (n_sel, n_rows), 1)
    ov_t = ((n_i * CMP_STRIDE <= s_i * SEL_BLOCK + (SEL_BLOCK - 1))
            & (n_i * CMP_STRIDE + (CMP_BLOCK - 1) >= s_i * SEL_BLOCK)).astype(BF16)

    blk = lax.broadcasted_iota(jnp.int32, (n_sel, tq), 0)
    t_lane = qi * tq + lax.broadcasted_iota(jnp.int32, (n_sel, tq), 1)
    cur = t_lane // SEL_BLOCK
    forced = (blk == 0) | (blk == cur) | (blk == cur - 1)
    future = blk > cur

    for g in range(NSA_KV_HEADS):
        qg = jnp.concatenate([q[:, _head_lanes(g, i)] for i in range(NSA_GROUP)], axis=0)
        s_t = _qk(kc_ref[g], qg)
        ps = []
        for i in range(NSA_GROUP):
            s = jnp.where(mask_c, s_t[:, i * tq:(i + 1) * tq], NEG_INF)
            e = jnp.exp2(s - jnp.max(s, axis=0, keepdims=True))
            ps.append(e / jnp.sum(e, axis=0, keepdims=True) * maskf)
        o_t = jnp.dot(vct_ref[g], jnp.concatenate(ps, axis=1).astype(BF16),
                      preferred_element_type=F32)
        for j in range(NSA_GROUP // 2):
            pair = jnp.concatenate([o_t[:, (2 * j) * tq:(2 * j + 1) * tq],
                                    o_t[:, (2 * j + 1) * tq:(2 * j + 2) * tq]], axis=0)
            ocmp_ref[g * (NSA_GROUP // 2) + j] = pair.T.astype(ocmp_ref.dtype)

        psum = ps[0] + ps[1] + ps[2] + ps[3]
        imp_t = jnp.zeros((n_sel, tq), F32)
        for piece in _split3(psum):
            imp_t = imp_t + jnp.dot(ov_t, piece, preferred_element_type=F32)
        val = jnp.where(forced, jnp.inf, jnp.where(future, -jnp.inf, imp_t))
        rank = jnp.zeros((n_sel, tq), F32)
        for sp in range(n_sel):
            other = val[sp:sp + 1, :]
            beats = (other > val) | ((other == val) & (sp < blk))
            rank = rank + jnp.where(beats, 1.0, 0.0)
        pen = jnp.where((rank < float(SEL_TOPN)) & jnp.logical_not(future), 0.0, NEG_INF)
        base = NSA_EXTRA_BASE[g]
        rows = [jnp.zeros((base, tq), F32)] if base else []
        rows += [pen, jnp.zeros((LANES - base - n_sel, tq), F32)]
        selx_ref[g] = jnp.concatenate(rows, axis=0).T.astype(selx_ref.dtype)


def _cmp_select(h3, kc, vct):
    batch, seq, _ = h3.shape
    tq = CMP_TQ
    n_rows = kc.shape[2]
    n_pairs = NSA_HEADS // 2
    assert 4 * (seq // SEL_BLOCK) == LANES and 2 * HEAD_DIM == LANES
    kv_spec = lambda a: pl.BlockSpec((None,) + a.shape[1:], lambda b, i: (b, 0, 0, 0))
    return pl.pallas_call(
        _cmp_select_kernel,
        grid=(batch, seq // tq),
        in_specs=[pl.BlockSpec((None, tq, NSA_WIDTH), lambda b, i: (b, i, OFF_NQ // NSA_WIDTH)),
                  kv_spec(kc), kv_spec(vct)],
        out_specs=[pl.BlockSpec((None, n_pairs, tq, LANES), lambda b, i: (b, 0, i, 0)),
                   pl.BlockSpec((None, NSA_KV_HEADS, tq, LANES), lambda b, i: (b, 0, i, 0))],
        out_shape=[jax.ShapeDtypeStruct((batch, n_pairs, seq, LANES), BF16),
                   jax.ShapeDtypeStruct((batch, NSA_KV_HEADS, seq, LANES), BF16)],
        compiler_params=_params("parallel", "parallel"),
        name="cmp_select",
    )(h3, kc, vct)


def _nsa_kernel(q_ref, sk_ref, sv_ref, wk_ref, wv_ref, gate_ref, selx_ref, ocmp_ref, ext_ref, srow_ref,
                mask_ref, gexp_ref, o_ref, ksel_ref, vsel_ref, kwin_ref, vwin_ref, ssel_ref, swin_ref):
    step = q_ref.shape[0]
    seq = sk_ref.shape[0]
    tq = ATT_TQ
    qi = pl.program_id(1)
    win_tiles = WINDOW // tq + 1
    lane = lax.broadcasted_iota(jnp.int32, (1, LANES), 1)
    halves = (lane < HEAD_DIM, lane >= HEAD_DIM)

    @pl.when(qi == 0)
    def _():
        for g in range(NSA_KV_HEADS):
            ksel_ref[g] = jnp.where(halves[g], sk_ref[...], ext_ref[g])
            kwin_ref[g] = jnp.where(halves[g], wk_ref[...], ext_ref[g])
            vsel_ref[g] = jnp.where(halves[g], sv_ref[...], jnp.ones((seq, LANES), BF16))
            vwin_ref[g] = jnp.where(halves[g], wv_ref[...], jnp.ones((seq, LANES), BF16))

    def group_sweep(step_idx, g, carry):
        half = (lane // HEAD_DIM) == g
        is_g0 = g == 0
        q = q_ref[...]
        pen = selx_ref[g]
        diag, tail = mask_ref[0], mask_ref[1]
        gsig = jax.nn.sigmoid(gate_ref[...].astype(F32)).astype(BF16)
        for r in range(step // tq):
            rsl = slice(r * tq, (r + 1) * tq)
            q_sel, q_win = [], []
            for i in range(NSA_GROUP):
                blk = q[rsl, i * LANES:(i + 1) * LANES]
                srow = jnp.broadcast_to(srow_ref[g * NSA_GROUP + i], blk.shape)
                q_sel.append(jnp.where(half, blk, pen[rsl] + srow))
                q_win.append(jnp.where(half, blk, srow))
            q_sel = jnp.concatenate(q_sel, axis=0)
            q_win = jnp.concatenate(q_win, axis=0)

            n_tiles = (step_idx * step + (r + 1) * tq) // LANES
            sbuf, wbuf = ssel_ref.at[r % SCORE_BUFS], swin_ref.at[r % SCORE_BUFS]
            if n_tiles < win_tiles:
                w_start, w_tiles, w_masks = 0, n_tiles, {n_tiles - 1: diag}
            else:
                w_start, w_tiles = (n_tiles - win_tiles) * LANES, win_tiles
                w_masks = {0: tail, win_tiles - 1: diag}
            max_s = _stage_scores(q_sel, ksel_ref.at[g], sbuf, 0, n_tiles, {n_tiles - 1: diag})
            max_w = _stage_scores(q_win, kwin_ref.at[g], wbuf, w_start, w_tiles, w_masks)
            acc_s = _softmax_pv(sbuf, max_s, vsel_ref.at[g], 0, n_tiles)
            acc_w = _softmax_pv(wbuf, max_w, vwin_ref.at[g], w_start, w_tiles)
            o_sel = acc_s / pltpu.roll(acc_s, HEAD_DIM, 1)
            o_win = acc_w / pltpu.roll(acc_w, HEAD_DIM, 1)
            gate = [jnp.dot(gsig[rsl], gexp_ref[g, br], preferred_element_type=F32) for br in range(3)]
            for j in range(NSA_GROUP // 2):
                first = slice((2 * j) * tq, (2 * j + 1) * tq)
                second = slice((2 * j + 1) * tq, (2 * j + 2) * tq)

                def head_pair(o):
                    keep = jnp.where(is_g0, o[first], o[second])
                    move = jnp.where(is_g0, o[second], o[first])
                    return jnp.where(half, keep, pltpu.roll(move, HEAD_DIM, 1))

                cols = slice(j * LANES, (j + 1) * LANES)
                blk_idx = g * (NSA_GROUP // 2) + j
                out = (gate[0][:, cols] * ocmp_ref[blk_idx, rsl, :].astype(F32)
                       + gate[1][:, cols] * head_pair(o_sel) + gate[2][:, cols] * head_pair(o_win))
                o_ref[blk_idx, rsl, :] = out.astype(o_ref.dtype)
        return carry

    for step_idx in range(seq // step):
        @pl.when(qi == step_idx)
        def _(step_idx=step_idx):
            lax.fori_loop(0, NSA_KV_HEADS, functools.partial(group_sweep, step_idx), 0)


def _nsa_attn(h3, selx, ocmp, slopes):
    batch, seq, _ = h3.shape
    tq = NSA_STEP
    assert seq % tq == 0 and tq % ATT_TQ == 0 and ATT_TQ == LANES and WINDOW % ATT_TQ == 0
    n_sel = seq // SEL_BLOCK
    rows = NSA_GROUP * ATT_TQ
    win_keys = WINDOW + ATT_TQ

    bias = _bias_columns(seq)
    onehot = (np.arange(seq)[:, None] // SEL_BLOCK == np.arange(n_sel)[None, :]).astype(np.float32)
    ext = np.zeros((NSA_KV_HEADS, seq, LANES), np.float32)
    srow = np.zeros((NSA_HEADS, 1, LANES), np.float32)
    slope_cols = _slope_columns(slopes[DIFF_HEADS:])
    for g in range(NSA_KV_HEADS):
        base = NSA_EXTRA_BASE[g]
        ext[g, :, base:base + n_sel] = onehot
        ext[g, :, base + n_sel:base + n_sel + N_BIAS_COLS] = bias
        srow[g * NSA_GROUP:(g + 1) * NSA_GROUP, 0, base + n_sel:base + n_sel + N_BIAS_COLS] = \
            slope_cols[g * NSA_GROUP:(g + 1) * NSA_GROUP]
    masks = _causal_stack(rows, LANES, ATT_TQ)[[1, 3]]
    gexp = np.zeros((NSA_KV_HEADS, 3, LANES, NSA_GROUP * HEAD_DIM), np.float32)
    for g in range(NSA_KV_HEADS):
        for br in range(3):
            for i in range(NSA_GROUP):
                gexp[g, br, 3 * (g * NSA_GROUP + i) + br, i * HEAD_DIM:(i + 1) * HEAD_DIM] = 1.0

    n_pairs = NSA_HEADS // 2
    kvspec = lambda off: pl.BlockSpec((None, seq, LANES), lambda b, i: (b, 0, off // LANES))
    tile = lambda w, blk: pl.BlockSpec((None, tq, w), lambda b, i: (b, i, blk))
    blocks = lambda n: pl.BlockSpec((None, n, tq, LANES), lambda b, i: (b, 0, i, 0))
    const = lambda a: pl.BlockSpec(a.shape, lambda b, i: (0,) * a.ndim)
    return pl.pallas_call(
        _nsa_kernel,
        grid=(batch, seq // tq),
        in_specs=[tile(NSA_WIDTH, OFF_NQ // NSA_WIDTH),
                  kvspec(OFF_SK), kvspec(OFF_SV), kvspec(OFF_WK), kvspec(OFF_WV),
                  tile(LANES, OFF_G // LANES), blocks(NSA_KV_HEADS), blocks(n_pairs),
                  const(ext), const(srow), const(masks), const(gexp)],
        out_specs=blocks(n_pairs),
        out_shape=jax.ShapeDtypeStruct((batch, n_pairs, seq, LANES), BF16),
        scratch_shapes=[pltpu.VMEM((NSA_KV_HEADS, seq, LANES), BF16) for _ in range(4)]
                       + [pltpu.VMEM((SCORE_BUFS, rows, seq), F32),
                          pltpu.VMEM((SCORE_BUFS, rows, win_keys), F32)],
        compiler_params=_params("parallel", "arbitrary"),
        name="nsa_attn",
    )(h3, h3, h3, h3, h3, h3, selx, ocmp, jnp.asarray(ext, BF16), jnp.asarray(srow, BF16),
      jnp.asarray(masks), jnp.asarray(gexp, BF16))


def _layer_norm(y, g, b):
    mu = jnp.mean(y, axis=-1, keepdims=True)
    yc = y - mu
    var = jnp.mean(yc * yc, axis=-1, keepdims=True)
    return yc * lax.rsqrt(var + LN_EPS) * g + b


def _out_ffn_kernel(od_ref, on_ref, x_ref, wo_ref, g1_ref, b1_ref, wg_ref, wu_ref, wd_ref,
                    g2_ref, b2_ref, o_ref):
    for r0 in range(0, x_ref.shape[0], FFN_CHAIN):
        rows = slice(r0, r0 + FFN_CHAIN)
        att = jnp.concatenate([od_ref[rows, :]] + [on_ref[j, rows, :] for j in range(on_ref.shape[0])], axis=1)
        mix = jnp.dot(att, wo_ref[...], preferred_element_type=F32)
        x1 = _layer_norm(DEEPNORM_ALPHA * x_ref[rows, :] + mix, g1_ref[...], b1_ref[...])
        x1b = x1.astype(BF16)
        acc = jnp.zeros(x1.shape, F32)
        for c0 in range(0, D_FF, FFN_TF):
            gate = jnp.dot(x1b, wg_ref[:, c0:c0 + FFN_TF], preferred_element_type=F32)
            up = jnp.dot(x1b, wu_ref[:, c0:c0 + FFN_TF], preferred_element_type=F32)
            act = (jax.nn.silu(gate) * up).astype(BF16)
            acc = acc + jnp.dot(act, wd_ref[c0:c0 + FFN_TF, :], preferred_element_type=F32)
        o_ref[rows, :] = _layer_norm(DEEPNORM_ALPHA * x1 + acc, g2_ref[...], b2_ref[...])


def _out_ffn(od, on, x2, wo, g1, b1, wg, wu, wd, g2, b2):
    m = x2.shape[0]
    tm = FFN_TM
    _, n_pairs, seq, _ = on.shape
    per_seq = seq // tm
    const = lambda shape: pl.BlockSpec(shape, lambda i: (0, 0), pipeline_mode=pl.Buffered(1))
    return pl.pallas_call(
        _out_ffn_kernel,
        grid=(m // tm,),
        in_specs=[pl.BlockSpec((tm, DIFF_WIDTH), lambda i: (i, 0)),
                  pl.BlockSpec((None, n_pairs, tm, LANES), lambda i: (i // per_seq, 0, i % per_seq, 0)),
                  pl.BlockSpec((tm, D_MODEL), lambda i: (i, 0)),
                  const(wo.shape), const(g1.shape), const(b1.shape),
                  const(wg.shape), const(wu.shape), const(wd.shape),
                  const(g2.shape), const(b2.shape)],
        out_specs=pl.BlockSpec((tm, D_MODEL), lambda i: (i, 0)),
        out_shape=jax.ShapeDtypeStruct((m, D_MODEL), F32),
        compiler_params=_params("parallel"),
        name="out_ffn",
    )(od, on, x2, wo, g1, b1, wg, wu, wd, g2, b2)


def kernel(x, w_in, diff_lq1, diff_lk1, diff_lq2, diff_lk2, diff_subln_g, cmp_pe_k, cmp_w1_k, cmp_w2_k,
           cmp_pe_v, cmp_w1_v, cmp_w2_v, w_out, ln1_g, ln1_b, w_gate, w_up, w_down, ln2_g, ln2_b):
    batch, seq, _ = x.shape
    assert w_in.shape[0] == DEPTH
    assert seq % DIFF_STEP == 0 and seq % NSA_STEP == 0 and seq % CMP_TQ == 0
    assert (batch * seq) % PROJ_TM == 0 and (batch * seq) % FFN_TM == 0
    slopes = _alibi_slopes()
    x2 = x.reshape(batch * seq, D_MODEL)

    w = w_in[0]
    head_order = [g * NSA_GROUP + i for i in range(NSA_GROUP) for g in range(NSA_KV_HEADS)]
    w_nq = w[:, OFF_NQ:OFF_CK].reshape(D_MODEL, NSA_HEADS, HEAD_DIM)[:, head_order, :]
    w_pad = jnp.concatenate([w[:, :OFF_NQ], w_nq.reshape(D_MODEL, NSA_WIDTH), w[:, OFF_CK:],
                             jnp.zeros((D_MODEL, N_IN_PAD - N_IN), w.dtype)], axis=1).astype(BF16)
    h, hck, hcv = _in_proj(x2, w_pad)
    h3 = h.reshape(batch, seq, N_IN_PAD)

    kc, vc = _compress(hck, hcv, cmp_pe_k[0], cmp_w1_k[0], cmp_w2_k[0],
                       cmp_pe_v[0], cmp_w1_v[0], cmp_w2_v[0], batch, seq)
    o_diff = _diff_attn(h3, diff_lq1, diff_lk1, diff_lq2, diff_lk2, diff_subln_g, slopes)
    o_cmp, selx = _cmp_select(h3, kc, vc)
    o_nsa = _nsa_attn(h3, selx, o_cmp, slopes)

    assert seq % FFN_TM == 0
    out = _out_ffn(o_diff.reshape(batch * seq, DIFF_WIDTH), o_nsa, x2,
                   w_out[0].astype(BF16), ln1_g, ln1_b,
                   w_gate[0].astype(BF16), w_up[0].astype(BF16), w_down[0].astype(BF16), ln2_g, ln2_b)
    return out.reshape(batch, seq, D_MODEL)
```

```python
import functools
import math

import numpy as np
import jax
import jax.numpy as jnp
from jax import lax
from jax.experimental import pallas as pl
from jax.experimental.pallas import tpu as pltpu

F32 = jnp.float32
BF16 = jnp.bfloat16

D_MODEL = 1024
HEAD_DIM = 64
DIFF_HEADS = 4
DIFF_WIDTH = DIFF_HEADS * 2 * HEAD_DIM
NSA_HEADS = 8
NSA_KV_HEADS = 2
NSA_GROUP = NSA_HEADS // NSA_KV_HEADS
NSA_WIDTH = NSA_HEADS * HEAD_DIM
NSA_KV_WIDTH = NSA_KV_HEADS * HEAD_DIM
CMP_BLOCK = 32
CMP_STRIDE = 16
CMP_HIDDEN = 128
SEL_BLOCK = 64
SEL_TOPN = 16
WINDOW = 512
N_GATES = 3 * NSA_HEADS
D_FF = 2816
N_ALIBI = DIFF_HEADS + NSA_HEADS
LN_EPS = 1e-5
RMS_EPS = 1e-5
NEG_INF = -1e30
DEPTH = 1
DEEPNORM_ALPHA = (2.0 * DEPTH) ** 0.25
SCALE = HEAD_DIM ** -0.5
LOG2E = 1.4426950408889634
Q_SCALE = SCALE * LOG2E

OFF_DQ = 0
OFF_DK = OFF_DQ + DIFF_WIDTH
OFF_DV = OFF_DK + DIFF_WIDTH
OFF_NQ = OFF_DV + DIFF_WIDTH
OFF_CK = OFF_NQ + NSA_WIDTH
OFF_CV = OFF_CK + NSA_KV_WIDTH
OFF_SK = OFF_CV + NSA_KV_WIDTH
OFF_SV = OFF_SK + NSA_KV_WIDTH
OFF_WK = OFF_SV + NSA_KV_WIDTH
OFF_WV = OFF_WK + NSA_KV_WIDTH
OFF_G = OFF_WV + NSA_KV_WIDTH
N_IN = OFF_G + N_GATES

LANES = 128
N_IN_PAD = -(-N_IN // LANES) * LANES
VMEM_LIMIT = 56 * 1024 * 1024

PROJ_TM = 512
PROJ_TN = 256
CMP_TQ = 256
DIFF_STEP = 2048
NSA_STEP = 1024
ATT_TQ = 128
DIFF_TQ = 128
DIFF_HEADS_PER_STEP = 4
SCORE_BUFS = 4
DIFF_SCORE_BUFS = 2
KV_CHUNK = 512
FFN_TM = 1024
FFN_CHAIN = 512
FFN_TF = 256

POS_HI_UNIT = 256
N_PIECES = 3
N_BIAS_COLS = 2 * N_PIECES
NSA_EXTRA_BASE = (HEAD_DIM, 0)


def _lambda_init(layer_idx):
    return 0.8 - 0.6 * math.exp(-0.3 * layer_idx)


def _alibi_slopes():
    return np.asarray(2.0 ** (-8.0 * (np.arange(N_ALIBI) + 1) / N_ALIBI), dtype=np.float32)


def _params(*sem):
    return pltpu.CompilerParams(dimension_semantics=sem, vmem_limit_bytes=VMEM_LIMIT)


def _bf16_pieces(x):
    rest = np.asarray(x, np.float32)
    out = []
    for _ in range(N_PIECES):
        piece = rest.astype(BF16).astype(np.float32)
        out.append(piece)
        rest = rest - piece
    return out


def _bias_columns(seq):
    t = np.arange(seq)
    hi = ((t // POS_HI_UNIT) * POS_HI_UNIT).astype(np.float32)
    lo = (t % POS_HI_UNIT).astype(np.float32)
    return np.stack([hi] * N_PIECES + [lo] * N_PIECES, axis=1)


def _slope_columns(slopes):
    pieces = _bf16_pieces(np.asarray(slopes, np.float32) * np.float32(LOG2E))
    return np.stack(pieces + pieces, axis=1)


def _causal_stack(rows, width, period):
    r = (np.arange(rows) % period)[:, None]
    c = np.arange(width)[None, :]
    zero = np.zeros((rows, width), np.float32)
    diag = np.where(c <= r, 0.0, NEG_INF).astype(np.float32)
    full = np.full((rows, width), NEG_INF, np.float32)
    tail = np.where(c > r, 0.0, NEG_INF).astype(np.float32)
    return np.stack([zero, diag, full, tail])


def _qk(q, k):
    return lax.dot_general(q, k, (((1,), (1,)), ((), ())), preferred_element_type=F32)


def _stage_scores(q_aug, k_ref, score_ref, key_start, n_tiles, masks):
    rows = q_aug.shape[0]
    per_chunk = KV_CHUNK // LANES
    mx = jnp.full((rows, LANES), NEG_INF, F32)
    for c0 in range(0, n_tiles, per_chunk):
        c1 = min(c0 + per_chunk, n_tiles)
        s = _qk(q_aug, k_ref[key_start + c0 * LANES:key_start + c1 * LANES, :])
        for t in range(c0, c1):
            tile = s[:, (t - c0) * LANES:(t - c0 + 1) * LANES]
            if t in masks:
                tile = tile + masks[t]
            score_ref[:, t * LANES:(t + 1) * LANES] = tile
            mx = jnp.maximum(mx, tile)
    return mx


def _softmax_pv(score_ref, lane_max, v_ref, key_start, n_tiles):
    mb = jnp.broadcast_to(jnp.max(lane_max, axis=-1, keepdims=True), lane_max.shape)
    ps = [jnp.exp2(score_ref[:, t * LANES:(t + 1) * LANES] - mb).astype(BF16) for t in range(n_tiles)]
    v = v_ref[key_start:key_start + n_tiles * LANES, :]
    return jnp.dot(jnp.concatenate(ps, axis=1), v, preferred_element_type=F32)


def _in_proj_kernel(x_ref, w_ref, h_ref, hck_ref, hcv_ref):
    xb = x_ref[...].astype(BF16)
    for c0 in range(0, N_IN_PAD, PROJ_TN):
        c1 = min(c0 + PROJ_TN, N_IN_PAD)
        r = jnp.dot(xb, w_ref[:, c0:c1], preferred_element_type=F32)
        if c0 < OFF_DK or OFF_NQ <= c0 < OFF_CK:
            r = r * Q_SCALE
        h_ref[:, c0:c1] = r.astype(BF16)
        if c0 == OFF_CK:
            hck_ref[...] = r[:, :NSA_KV_WIDTH]
            hcv_ref[...] = r[:, NSA_KV_WIDTH:]


def _in_proj(x2, w_pad):
    m = x2.shape[0]
    return pl.pallas_call(
        _in_proj_kernel,
        grid=(m // PROJ_TM,),
        in_specs=[pl.BlockSpec((PROJ_TM, D_MODEL), lambda i: (i, 0)),
                  pl.BlockSpec((D_MODEL, N_IN_PAD), lambda i: (0, 0))],
        out_specs=[pl.BlockSpec((PROJ_TM, N_IN_PAD), lambda i: (i, 0)),
                   pl.BlockSpec((PROJ_TM, NSA_KV_WIDTH), lambda i: (i, 0)),
                   pl.BlockSpec((PROJ_TM, NSA_KV_WIDTH), lambda i: (i, 0))],
        out_shape=[jax.ShapeDtypeStruct((m, N_IN_PAD), BF16),
                   jax.ShapeDtypeStruct((m, NSA_KV_WIDTH), F32),
                   jax.ShapeDtypeStruct((m, NSA_KV_WIDTH), F32)],
        compiler_params=_params("parallel"),
        name="in_proj",
    )(x2, w_pad)


def _compress_kernel(hck_ref, hcv_ref, pek_ref, w1k_ref, w2k_ref, pev_ref, w1v_ref, w2v_ref,
                     kc_ref, vc_ref):
    n_rows = hck_ref.shape[0] // CMP_STRIDE
    half = CMP_BLOCK // 2
    for src, pe_ref, w1_ref, w2_ref, out_ref in ((hck_ref, pek_ref, w1k_ref, w2k_ref, kc_ref),
                                                   (hcv_ref, pev_ref, w1v_ref, w2v_ref, vc_ref)):
        acc_a = jnp.zeros((NSA_KV_HEADS * n_rows, CMP_HIDDEN), F32)
        acc_b = jnp.zeros((NSA_KV_HEADS * n_rows, CMP_HIDDEN), F32)
        for l in range(half):
            xl = src[pl.ds(l, n_rows, stride=CMP_STRIDE), :]
            xs = jnp.concatenate([xl[:, g * HEAD_DIM:(g + 1) * HEAD_DIM]
                                  for g in range(NSA_KV_HEADS)], axis=0)
            xa = (xs + pe_ref[l:l + 1, :]).astype(BF16)
            xb = (xs + pe_ref[half + l:half + l + 1, :]).astype(BF16)
            wa = w1_ref[l * HEAD_DIM:(l + 1) * HEAD_DIM, :].astype(BF16)
            wb = w1_ref[(half + l) * HEAD_DIM:(half + l + 1) * HEAD_DIM, :].astype(BF16)
            acc_a = acc_a + jnp.dot(xa, wa, preferred_element_type=F32)
            acc_b = acc_b + jnp.dot(xb, wb, preferred_element_type=F32)
        w2 = w2_ref[...].astype(BF16)
        for g in range(NSA_KV_HEADS):
            a = acc_a[g * n_rows:(g + 1) * n_rows]
            b = acc_b[g * n_rows:(g + 1) * n_rows]
            hid = a + pltpu.roll(b, n_rows - 1, 0)
            act = jax.nn.gelu(hid).astype(BF16)
            if out_ref is kc_ref:
                out = jnp.dot(act, w2, preferred_element_type=F32)
                out_ref[:, g * HEAD_DIM:(g + 1) * HEAD_DIM] = out.astype(out_ref.dtype)
            else:
                out_ref[g] = _qk(w2, act).astype(out_ref.dtype)


def _compress(hck, hcv, pe_k, w1_k, w2_k, pe_v, w1_v, w2_v, batch, seq):
    n_rows = seq // CMP_STRIDE
    w2_vt = w2_v.T
    full = lambda shape: pl.BlockSpec(shape, lambda b: (0,) * len(shape))
    kv_spec = pl.BlockSpec((seq, NSA_KV_WIDTH), lambda b: (b, 0))
    return pl.pallas_call(
        _compress_kernel,
        grid=(batch,),
        in_specs=[kv_spec, kv_spec,
                  full(pe_k.shape), full(w1_k.shape), full(w2_k.shape),
                  full(pe_v.shape), full(w1_v.shape), full(w2_vt.shape)],
        out_specs=[pl.BlockSpec((None, n_rows, NSA_KV_WIDTH), lambda b: (b, 0, 0)),
                   pl.BlockSpec((None, NSA_KV_HEADS, HEAD_DIM, n_rows), lambda b: (b, 0, 0, 0))],
        out_shape=[jax.ShapeDtypeStruct((batch, n_rows, NSA_KV_WIDTH), BF16),
                   jax.ShapeDtypeStruct((batch, NSA_KV_HEADS, HEAD_DIM, n_rows), BF16)],
        compiler_params=_params("parallel"),
        name="compress",
    )(hck, hcv, pe_k, w1_k, w2_k, pe_v, w1_v, w2_vt)


def _diff_kernel(q_ref, k_ref, v_ref, pos_ref, srow_ref, mask_ref, lq1_ref, lk1_ref, lq2_ref, lk2_ref,
                 g_ref, o_ref, kaug_ref, vaug_ref, s_ref, *, lam_init):
    step = q_ref.shape[0]
    seq = k_ref.shape[0]
    dv = 2 * HEAD_DIM
    n_heads = q_ref.shape[1] // LANES
    qi = pl.program_id(2)
    low = lax.broadcasted_iota(jnp.int32, (1, LANES), 1) < HEAD_DIM
    halves = (low, jnp.logical_not(low))
    head_lanes = [slice(hh * LANES, (hh + 1) * LANES) for hh in range(n_heads)]

    @pl.when(qi == 0)
    def _():
        for hh in range(n_heads):
            k = k_ref[:, head_lanes[hh]]
            for c in range(2):
                kaug_ref[hh, c] = jnp.where(halves[c], k, pos_ref[...])
            vaug_ref[hh, :, :dv] = v_ref[:, head_lanes[hh]]
            vaug_ref[hh, :, dv:] = jnp.ones((seq, dv), BF16)

    q_aug = []
    for hh in range(n_heads):
        q = q_ref[:, head_lanes[hh]]
        srow = jnp.broadcast_to(srow_ref[hh:hh + 1, :], q.shape)
        q_aug.append([jnp.where(halves[c], q, srow) for c in range(2)])
    lam = (jnp.exp(jnp.sum(lq1_ref[...] * lk1_ref[...], axis=-1, keepdims=True))
           - jnp.exp(jnp.sum(lq2_ref[...] * lk2_ref[...], axis=-1, keepdims=True)) + lam_init)

    tq = mask_ref.shape[1]
    per_chain = tq // LANES

    def sweep(step_idx):
        for r in range(step // tq):
            rows = slice(r * tq, (r + 1) * tq)
            n_tiles = (step_idx * step + (r + 1) * tq) // LANES
            masks = {n_tiles - per_chain + u: mask_ref[u] for u in range(per_chain)}
            for hh in range(n_heads):
                outs = []
                for c in range(2):
                    buf = s_ref.at[hh * 2 + c, r % DIFF_SCORE_BUFS]
                    lane_max = _stage_scores(q_aug[hh][c][rows], kaug_ref.at[hh, c], buf, 0, n_tiles, masks)
                    acc = _softmax_pv(buf, lane_max, vaug_ref.at[hh], 0, n_tiles)
                    outs.append(acc[:, :dv] / acc[:, dv:])
                o = outs[0] - lam * outs[1]
                o = o * lax.rsqrt(jnp.mean(o * o, axis=-1, keepdims=True) + RMS_EPS) * g_ref[...]
                o_ref[rows, head_lanes[hh]] = (o * (1.0 - lam_init)).astype(o_ref.dtype)

    for step_idx in range(seq // step):
        pl.when(qi == step_idx)(functools.partial(sweep, step_idx))


def _diff_attn(h3, lq1, lk1, lq2, lk2, subln_g, slopes):
    batch, seq, _ = h3.shape
    tq = DIFF_STEP
    assert seq % tq == 0 and tq % DIFF_TQ == 0 and DIFF_TQ % LANES == 0
    assert DIFF_HEADS % DIFF_HEADS_PER_STEP == 0 and (OFF_DK // LANES) % DIFF_HEADS_PER_STEP == 0
    kb, vb = OFF_DK // LANES, OFF_DV // LANES
    dv = 2 * HEAD_DIM

    bias = _bias_columns(seq)
    pos = np.zeros((seq, LANES), np.float32)
    srow = np.zeros((DIFF_HEADS, 1, LANES), np.float32)
    slope_cols = _slope_columns(slopes[:DIFF_HEADS])
    for base in (0, HEAD_DIM):
        pos[:, base:base + N_BIAS_COLS] = bias
        srow[:, 0, base:base + N_BIAS_COLS] = slope_cols
    masks = _causal_stack(DIFF_TQ, DIFF_TQ, DIFF_TQ)[1].reshape(DIFF_TQ, DIFF_TQ // LANES, LANES).transpose(1, 0, 2)

    hn = DIFF_HEADS_PER_STEP
    width = hn * LANES
    kb, vb = kb // hn, vb // hn
    vec = lambda n: pl.BlockSpec((1, n), lambda b, h, i: (0, 0))
    return pl.pallas_call(
        functools.partial(_diff_kernel, lam_init=_lambda_init(0)),
        grid=(batch, DIFF_HEADS // hn, seq // tq),
        in_specs=[pl.BlockSpec((None, tq, width), lambda b, h, i: (b, i, h)),
                  pl.BlockSpec((None, seq, width), lambda b, h, i: (b, 0, kb + h)),
                  pl.BlockSpec((None, seq, width), lambda b, h, i: (b, 0, vb + h)),
                  pl.BlockSpec((seq, LANES), lambda b, h, i: (0, 0)),
                  pl.BlockSpec((None, hn, LANES), lambda b, h, i: (h, 0, 0)),
                  pl.BlockSpec(masks.shape, lambda b, h, i: (0, 0, 0)),
                  vec(HEAD_DIM), vec(HEAD_DIM), vec(HEAD_DIM), vec(HEAD_DIM), vec(dv)],
        out_specs=pl.BlockSpec((None, tq, width), lambda b, h, i: (b, i, h)),
        out_shape=jax.ShapeDtypeStruct((batch, seq, DIFF_WIDTH), BF16),
        scratch_shapes=[pltpu.VMEM((hn, 2, seq, LANES), BF16),
                        pltpu.VMEM((hn, seq, 2 * dv), BF16),
                        pltpu.VMEM((hn * 2, DIFF_SCORE_BUFS, DIFF_TQ, seq), F32)],
        compiler_params=_params("parallel", "parallel", "arbitrary"),
        name="diff_attn",
    )(h3, h3, h3, jnp.asarray(pos, BF16), jnp.asarray(srow.reshape(DIFF_HEADS // hn, hn, LANES), BF16),
      jnp.asarray(masks),
      lq1, lk1, lq2, lk2, subln_g)


def _split3(x):
    hi = x.astype(BF16)
    r1 = x - hi.astype(F32)
    mid = r1.astype(BF16)
    lo = (r1 - mid.astype(F32)).astype(BF16)
    return hi, mid, lo


def _compressed_branch(q_cmp, kc_aug, vct, t0, need_rank):
    tq = q_cmp.shape[0] // NSA_GROUP
    n_rows = kc_aug.shape[0]
    n_sel = LANES // 4
    slot = lax.broadcasted_iota(jnp.int32, (n_rows, tq), 0)
    t_q = t0 + lax.broadcasted_iota(jnp.int32, (n_rows, tq), 1)
    mask_c = (slot * CMP_STRIDE + (CMP_BLOCK - 1)) <= t_q
    maskf = jnp.where(mask_c, 1.0, 0.0)

    s_t = _qk(kc_aug, q_cmp)
    ps = []
    for i in range(NSA_GROUP):
        s = jnp.where(mask_c, s_t[:, i * tq:(i + 1) * tq], NEG_INF)
        e = jnp.exp2(s - jnp.max(s, axis=0, keepdims=True))
        ps.append(e / jnp.sum(e, axis=0, keepdims=True) * maskf)
    o_t = jnp.dot(vct, jnp.concatenate(ps, axis=1).astype(BF16), preferred_element_type=F32)
    pairs = [jnp.concatenate([o_t[:, (2 * j) * tq:(2 * j + 1) * tq],
                              o_t[:, (2 * j + 1) * tq:(2 * j + 2) * tq]], axis=0).T
             for j in range(NSA_GROUP // 2)]

    blk = lax.broadcasted_iota(jnp.int32, (n_sel, tq), 0)
    cur = (t0 + lax.broadcasted_iota(jnp.int32, (n_sel, tq), 1)) // SEL_BLOCK
    future = blk > cur
    if not need_rank:
        keep = jnp.logical_not(future)
    else:
        s_i = lax.broadcasted_iota(jnp.int32, (n_sel, n_rows), 0)
        n_i = lax.broadcasted_iota(jnp.int32, (n_sel, n_rows), 1)
        ov_t = ((n_i * CMP_STRIDE <= s_i * SEL_BLOCK + (SEL_BLOCK - 1))
                & (n_i * CMP_STRIDE + (CMP_BLOCK - 1) >= s_i * SEL_BLOCK)).astype(BF16)
        psum = ps[0] + ps[1] + ps[2] + ps[3]
        imp_t = jnp.zeros((n_sel, tq), F32)
        for piece in _split3(psum):
            imp_t = imp_t + jnp.dot(ov_t, piece, preferred_element_type=F32)
        forced = (blk == 0) | (blk == cur) | (blk == cur - 1)
        val = jnp.where(forced, jnp.inf, jnp.where(future, -jnp.inf, imp_t))
        rank = jnp.zeros((n_sel, tq), F32)
        for sp in range(n_sel):
            other = val[sp:sp + 1, :]
            beats = (other > val) | ((other == val) & (sp < blk))
            rank = rank + jnp.where(beats, 1.0, 0.0)
        keep = (rank < float(SEL_TOPN)) & jnp.logical_not(future)
    return pairs, jnp.where(keep, 0.0, NEG_INF)


def _cmp_select_kernel(q_ref, kc_ref, vct_ref, ocmp_ref, selx_ref):
    tq = q_ref.shape[0]
    n_sel = LANES // 4
    qi = pl.program_id(1)
    n_plain = (SEL_TOPN * SEL_BLOCK) // tq

    def tile(need_rank):
        q = q_ref[...]
        for g in range(NSA_KV_HEADS):
            q_cmp = jnp.concatenate([q[:, i * LANES + g * HEAD_DIM:i * LANES + (g + 1) * HEAD_DIM]
                                     for i in range(NSA_GROUP)], axis=0)
            pairs, pen = _compressed_branch(q_cmp, kc_ref[:, g * HEAD_DIM:(g + 1) * HEAD_DIM], vct_ref[g],
                                            qi * tq, need_rank)
            for j in range(NSA_GROUP // 2):
                ocmp_ref[g * (NSA_GROUP // 2) + j] = pairs[j].astype(ocmp_ref.dtype)
            base = NSA_EXTRA_BASE[g]
            rows = [jnp.zeros((base, tq), F32)] if base else []
            rows += [pen, jnp.zeros((LANES - base - n_sel, tq), F32)]
            selx_ref[g] = jnp.concatenate(rows, axis=0).T.astype(selx_ref.dtype)

    pl.when(qi < n_plain)(functools.partial(tile, False))
    pl.when(qi >= n_plain)(functools.partial(tile, True))


def _cmp_select(h3, kc, vct):
    batch, seq, _ = h3.shape
    tq = CMP_TQ
    n_pairs = NSA_HEADS // 2
    assert 4 * (seq // SEL_BLOCK) == LANES and 2 * HEAD_DIM == LANES and (SEL_TOPN * SEL_BLOCK) % tq == 0
    return pl.pallas_call(
        _cmp_select_kernel,
        grid=(batch, seq // tq),
        in_specs=[pl.BlockSpec((None, tq, NSA_WIDTH), lambda b, i: (b, i, OFF_NQ // NSA_WIDTH)),
                  pl.BlockSpec((None,) + kc.shape[1:], lambda b, i: (b, 0, 0)),
                  pl.BlockSpec((None,) + vct.shape[1:], lambda b, i: (b, 0, 0, 0))],
        out_specs=[pl.BlockSpec((None, n_pairs, tq, LANES), lambda b, i: (b, 0, i, 0)),
                   pl.BlockSpec((None, NSA_KV_HEADS, tq, LANES), lambda b, i: (b, 0, i, 0))],
        out_shape=[jax.ShapeDtypeStruct((batch, n_pairs, seq, LANES), BF16),
                   jax.ShapeDtypeStruct((batch, NSA_KV_HEADS, seq, LANES), BF16)],
        compiler_params=_params("parallel", "parallel"),
        name="cmp_select",
    )(h3, kc, vct)


def _nsa_kernel(q_ref, sk_ref, sv_ref, wk_ref, wv_ref, gate_ref, selx_ref, ocmp_ref, ext_ref, srow_ref,
                mask_ref, gexp_ref, o_ref, ksel_ref, vsel_ref, kwin_ref, vwin_ref, ssel_ref, swin_ref):
    step = q_ref.shape[0]
    seq = sk_ref.shape[0]
    tq = ATT_TQ
    qi = pl.program_id(1)
    win_tiles = WINDOW // tq + 1
    lane = lax.broadcasted_iota(jnp.int32, (1, LANES), 1)
    halves = (lane < HEAD_DIM, lane >= HEAD_DIM)

    @pl.when(qi == 0)
    def _():
        for g in range(NSA_KV_HEADS):
            ksel_ref[g] = jnp.where(halves[g], sk_ref[...], ext_ref[g])
            kwin_ref[g] = jnp.where(halves[g], wk_ref[...], ext_ref[g])
            vsel_ref[g] = jnp.where(halves[g], sv_ref[...], jnp.ones((seq, LANES), BF16))
            vwin_ref[g] = jnp.where(halves[g], wv_ref[...], jnp.ones((seq, LANES), BF16))

    def group_sweep(step_idx, g, carry):
        half = (lane // HEAD_DIM) == g
        is_g0 = g == 0
        q = q_ref[...]
        pen = selx_ref[g]
        diag, tail = mask_ref[0], mask_ref[1]
        gsig = jax.nn.sigmoid(gate_ref[...].astype(F32)).astype(BF16)
        for r in range(step // tq):
            rsl = slice(r * tq, (r + 1) * tq)
            q_sel, q_win = [], []
            for i in range(NSA_GROUP):
                blk = q[rsl, i * LANES:(i + 1) * LANES]
                srow = jnp.broadcast_to(srow_ref[g * NSA_GROUP + i], blk.shape)
                q_sel.append(jnp.where(half, blk, pen[rsl] + srow))
                q_win.append(jnp.where(half, blk, srow))
            q_sel = jnp.concatenate(q_sel, axis=0)
            q_win = jnp.concatenate(q_win, axis=0)

            n_tiles = (step_idx * step + (r + 1) * tq) // LANES
            sbuf, wbuf = ssel_ref.at[r % SCORE_BUFS], swin_ref.at[r % SCORE_BUFS]
            if n_tiles < win_tiles:
                w_start, w_tiles, w_masks = 0, n_tiles, {n_tiles - 1: diag}
            else:
                w_start, w_tiles = (n_tiles - win_tiles) * LANES, win_tiles
                w_masks = {0: tail, win_tiles - 1: diag}
            max_s = _stage_scores(q_sel, ksel_ref.at[g], sbuf, 0, n_tiles, {n_tiles - 1: diag})
            max_w = _stage_scores(q_win, kwin_ref.at[g], wbuf, w_start, w_tiles, w_masks)
            acc_s = _softmax_pv(sbuf, max_s, vsel_ref.at[g], 0, n_tiles)
            acc_w = _softmax_pv(wbuf, max_w, vwin_ref.at[g], w_start, w_tiles)
            o_sel = acc_s / pltpu.roll(acc_s, HEAD_DIM, 1)
            o_win = acc_w / pltpu.roll(acc_w, HEAD_DIM, 1)
            gate = [jnp.dot(gsig[rsl], gexp_ref[g, br], preferred_element_type=F32) for br in range(3)]
            for j in range(NSA_GROUP // 2):
                first = slice((2 * j) * tq, (2 * j + 1) * tq)
                second = slice((2 * j + 1) * tq, (2 * j + 2) * tq)

                def head_pair(o):
                    keep = jnp.where(is_g0, o[first], o[second])
                    move = jnp.where(is_g0, o[second], o[first])
                    return jnp.where(half, keep, pltpu.roll(move, HEAD_DIM, 1))

                cols = slice(j * LANES, (j + 1) * LANES)
                blk_idx = g * (NSA_GROUP // 2) + j
                out = (gate[0][:, cols] * ocmp_ref[blk_idx, rsl, :].astype(F32)
                       + gate[1][:, cols] * head_pair(o_sel) + gate[2][:, cols] * head_pair(o_win))
                o_ref[blk_idx, rsl, :] = out.astype(o_ref.dtype)
        return carry

    for step_idx in range(seq // step):
        @pl.when(qi == step_idx)
        def _(step_idx=step_idx):
            lax.fori_loop(0, NSA_KV_HEADS, functools.partial(group_sweep, step_idx), 0)


def _nsa_attn(h3, selx, ocmp, slopes):
    batch, seq, _ = h3.shape
    tq = NSA_STEP
    assert seq % tq == 0 and tq % ATT_TQ == 0 and ATT_TQ == LANES and WINDOW % ATT_TQ == 0
    n_sel = seq // SEL_BLOCK
    rows = NSA_GROUP * ATT_TQ
    win_keys = WINDOW + ATT_TQ

    bias = _bias_columns(seq)
    onehot = (np.arange(seq)[:, None] // SEL_BLOCK == np.arange(n_sel)[None, :]).astype(np.float32)
    ext = np.zeros((NSA_KV_HEADS, seq, LANES), np.float32)
    srow = np.zeros((NSA_HEADS, 1, LANES), np.float32)
    slope_cols = _slope_columns(slopes[DIFF_HEADS:])
    for g in range(NSA_KV_HEADS):
        base = NSA_EXTRA_BASE[g]
        ext[g, :, base:base + n_sel] = onehot
        ext[g, :, base + n_sel:base + n_sel + N_BIAS_COLS] = bias
        srow[g * NSA_GROUP:(g + 1) * NSA_GROUP, 0, base + n_sel:base + n_sel + N_BIAS_COLS] = \
            slope_cols[g * NSA_GROUP:(g + 1) * NSA_GROUP]
    masks = _causal_stack(rows, LANES, ATT_TQ)[[1, 3]]
    gexp = np.zeros((NSA_KV_HEADS, 3, LANES, NSA_GROUP * HEAD_DIM), np.float32)
    for g in range(NSA_KV_HEADS):
        for br in range(3):
            for i in range(NSA_GROUP):
                gexp[g, br, 3 * (g * NSA_GROUP + i) + br, i * HEAD_DIM:(i + 1) * HEAD_DIM] = 1.0

    n_pairs = NSA_HEADS // 2
    kvspec = lambda off: pl.BlockSpec((None, seq, LANES), lambda b, i: (b, 0, off // LANES))
    tile = lambda w, blk: pl.BlockSpec((None, tq, w), lambda b, i: (b, i, blk))
    blocks = lambda n: pl.BlockSpec((None, n, tq, LANES), lambda b, i: (b, 0, i, 0))
    const = lambda a: pl.BlockSpec(a.shape, lambda b, i: (0,) * a.ndim)
    return pl.pallas_call(
        _nsa_kernel,
        grid=(batch, seq // tq),
        in_specs=[tile(NSA_WIDTH, OFF_NQ // NSA_WIDTH),
                  kvspec(OFF_SK), kvspec(OFF_SV), kvspec(OFF_WK), kvspec(OFF_WV),
                  tile(LANES, OFF_G // LANES), blocks(NSA_KV_HEADS), blocks(n_pairs),
                  const(ext), const(srow), const(masks), const(gexp)],
        out_specs=blocks(n_pairs),
        out_shape=jax.ShapeDtypeStruct((batch, n_pairs, seq, LANES), BF16),
        scratch_shapes=[pltpu.VMEM((NSA_KV_HEADS, seq, LANES), BF16) for _ in range(4)]
                       + [pltpu.VMEM((SCORE_BUFS, rows, seq), F32),
                          pltpu.VMEM((SCORE_BUFS, rows, win_keys), F32)],
        compiler_params=_params("parallel", "arbitrary"),
        name="nsa_attn",
    )(h3, h3, h3, h3, h3, h3, selx, ocmp, jnp.asarray(ext, BF16), jnp.asarray(srow, BF16),
      jnp.asarray(masks), jnp.asarray(gexp, BF16))


def _layer_norm(y, g, b):
    mu = jnp.mean(y, axis=-1, keepdims=True)
    yc = y - mu
    var = jnp.mean(yc * yc, axis=-1, keepdims=True)
    return yc * lax.rsqrt(var + LN_EPS) * g + b


def _out_ffn_kernel(od_ref, on_ref, x_ref, wo_ref, g1_ref, b1_ref, wg_ref, wu_ref, wd_ref,
                    g2_ref, b2_ref, o_ref):
    for r0 in range(0, x_ref.shape[0], FFN_CHAIN):
        rows = slice(r0, r0 + FFN_CHAIN)
        att = jnp.concatenate([od_ref[rows, :]] + [on_ref[j, rows, :] for j in range(on_ref.shape[0])], axis=1)
        mix = jnp.dot(att, wo_ref[...], preferred_element_type=F32)
        x1 = _layer_norm(DEEPNORM_ALPHA * x_ref[rows, :] + mix, g1_ref[...], b1_ref[...])
        x1b = x1.astype(BF16)
        acc = jnp.zeros(x1.shape, F32)
        for c0 in range(0, D_FF, FFN_TF):
            gate = jnp.dot(x1b, wg_ref[:, c0:c0 + FFN_TF], preferred_element_type=F32)
            up = jnp.dot(x1b, wu_ref[:, c0:c0 + FFN_TF], preferred_element_type=F32)
            act = (jax.nn.silu(gate) * up).astype(BF16)
            acc = acc + jnp.dot(act, wd_ref[c0:c0 + FFN_TF, :], preferred_element_type=F32)
        o_ref[rows, :] = _layer_norm(DEEPNORM_ALPHA * x1 + acc, g2_ref[...], b2_ref[...])


def _out_ffn(od, on, x2, wo, g1, b1, wg, wu, wd, g2, b2):
    m = x2.shape[0]
    tm = FFN_TM
    _, n_pairs, seq, _ = on.shape
    per_seq = seq // tm
    const = lambda shape: pl.BlockSpec(shape, lambda i: (0, 0), pipeline_mode=pl.Buffered(1))
    return pl.pallas_call(
        _out_ffn_kernel,
        grid=(m // tm,),
        in_specs=[pl.BlockSpec((tm, DIFF_WIDTH), lambda i: (i, 0)),
                  pl.BlockSpec((None, n_pairs, tm, LANES), lambda i: (i // per_seq, 0, i % per_seq, 0)),
                  pl.BlockSpec((tm, D_MODEL), lambda i: (i, 0)),
                  const(wo.shape), const(g1.shape), const(b1.shape),
                  const(wg.shape), const(wu.shape), const(wd.shape),
                  const(g2.shape), const(b2.shape)],
        out_specs=pl.BlockSpec((tm, D_MODEL), lambda i: (i, 0)),
        out_shape=jax.ShapeDtypeStruct((m, D_MODEL), F32),
        compiler_params=_params("parallel"),
        name="out_ffn",
    )(od, on, x2, wo, g1, b1, wg, wu, wd, g2, b2)


def kernel(x, w_in, diff_lq1, diff_lk1, diff_lq2, diff_lk2, diff_subln_g, cmp_pe_k, cmp_w1_k, cmp_w2_k,
           cmp_pe_v, cmp_w1_v, cmp_w2_v, w_out, ln1_g, ln1_b, w_gate, w_up, w_down, ln2_g, ln2_b):
    batch, seq, _ = x.shape
    assert w_in.shape[0] == DEPTH
    assert seq % DIFF_STEP == 0 and seq % NSA_STEP == 0
    assert (batch * seq) % PROJ_TM == 0 and (batch * seq) % FFN_TM == 0
    slopes = _alibi_slopes()
    x2 = x.reshape(batch * seq, D_MODEL)

    w = w_in[0]
    head_order = [g * NSA_GROUP + i for i in range(NSA_GROUP) for g in range(NSA_KV_HEADS)]
    w_nq = w[:, OFF_NQ:OFF_CK].reshape(D_MODEL, NSA_HEADS, HEAD_DIM)[:, head_order, :]
    w_pad = jnp.concatenate([w[:, :OFF_NQ], w_nq.reshape(D_MODEL, NSA_WIDTH), w[:, OFF_CK:],
                             jnp.zeros((D_MODEL, N_IN_PAD - N_IN), w.dtype)], axis=1).astype(BF16)
    h, hck, hcv = _in_proj(x2, w_pad)
    h3 = h.reshape(batch, seq, N_IN_PAD)

    kc, vct = _compress(hck, hcv, cmp_pe_k[0], cmp_w1_k[0], cmp_w2_k[0],
                        cmp_pe_v[0], cmp_w1_v[0], cmp_w2_v[0], batch, seq)
    o_diff = _diff_attn(h3, diff_lq1, diff_lk1, diff_lq2, diff_lk2, diff_subln_g, slopes)
    o_cmp, selx = _cmp_select(h3, kc, vct)
    o_nsa = _nsa_attn(h3, selx, o_cmp, slopes)

    assert seq % FFN_TM == 0
    out = _out_ffn(o_diff.reshape(batch * seq, DIFF_WIDTH), o_nsa, x2,
                   w_out[0].astype(BF16), ln1_g, ln1_b,
                   w_gate[0].astype(BF16), w_up[0].astype(BF16), w_down[0].astype(BF16), ln2_g, ln2_b)
    return out.reshape(batch, seq, D_MODEL)
```

```python
import functools
import math

import numpy as np
import jax
import jax.numpy as jnp
from jax import lax
from jax.experimental import pallas as pl
from jax.experimental.pallas import tpu as pltpu

F32 = jnp.float32
BF16 = jnp.bfloat16

D_MODEL = 1024
HEAD_DIM = 64
DIFF_HEADS = 4
DIFF_WIDTH = DIFF_HEADS * 2 * HEAD_DIM
NSA_HEADS = 8
NSA_KV_HEADS = 2
NSA_GROUP = NSA_HEADS // NSA_KV_HEADS
NSA_WIDTH = NSA_HEADS * HEAD_DIM
NSA_KV_WIDTH = NSA_KV_HEADS * HEAD_DIM
CMP_BLOCK = 32
CMP_STRIDE = 16
CMP_HIDDEN = 128
SEL_BLOCK = 64
SEL_TOPN = 16
WINDOW = 512
N_GATES = 3 * NSA_HEADS
D_FF = 2816
N_ALIBI = DIFF_HEADS + NSA_HEADS
LN_EPS = 1e-5
RMS_EPS = 1e-5
NEG_INF = -1e30
DEPTH = 1
DEEPNORM_ALPHA = (2.0 * DEPTH) ** 0.25
SCALE = HEAD_DIM ** -0.5
LOG2E = 1.4426950408889634
Q_SCALE = SCALE * LOG2E

OFF_DQ = 0
OFF_DK = OFF_DQ + DIFF_WIDTH
OFF_DV = OFF_DK + DIFF_WIDTH
OFF_NQ = OFF_DV + DIFF_WIDTH
OFF_CK = OFF_NQ + NSA_WIDTH
OFF_CV = OFF_CK + NSA_KV_WIDTH
OFF_SK = OFF_CV + NSA_KV_WIDTH
OFF_SV = OFF_SK + NSA_KV_WIDTH
OFF_WK = OFF_SV + NSA_KV_WIDTH
OFF_WV = OFF_WK + NSA_KV_WIDTH
OFF_G = OFF_WV + NSA_KV_WIDTH
N_IN = OFF_G + N_GATES

LANES = 128
N_IN_PAD = -(-N_IN // LANES) * LANES
VMEM_LIMIT = 56 * 1024 * 1024

PROJ_TM = 512
PROJ_TN = 256
CMP_TQ = 512
DIFF_STEP = 2048
NSA_STEP = 1024
ATT_TQ = 128
DIFF_TQ = 128
DIFF_HEADS_PER_STEP = 4
SCORE_BUFS = 4
DIFF_SCORE_BUFS = 2
KV_CHUNK = 512
FFN_TM = 1024
FFN_CHAIN = 512
FFN_TF = 256

POS_HI_UNIT = 256
N_PIECES = 3
N_BIAS_COLS = 2 * N_PIECES
NSA_EXTRA_BASE = (HEAD_DIM, 0)


def _lambda_init(layer_idx):
    return 0.8 - 0.6 * math.exp(-0.3 * layer_idx)


def _alibi_slopes():
    return np.asarray(2.0 ** (-8.0 * (np.arange(N_ALIBI) + 1) / N_ALIBI), dtype=np.float32)


def _params(*sem):
    return pltpu.CompilerParams(dimension_semantics=sem, vmem_limit_bytes=VMEM_LIMIT)


def _bf16_pieces(x):
    rest = np.asarray(x, np.float32)
    out = []
    for _ in range(N_PIECES):
        piece = rest.astype(BF16).astype(np.float32)
        out.append(piece)
        rest = rest - piece
    return out


def _bias_columns(seq):
    t = np.arange(seq)
    hi = ((t // POS_HI_UNIT) * POS_HI_UNIT).astype(np.float32)
    lo = (t % POS_HI_UNIT).astype(np.float32)
    return np.stack([hi] * N_PIECES + [lo] * N_PIECES, axis=1)


def _slope_columns(slopes):
    pieces = _bf16_pieces(np.asarray(slopes, np.float32) * np.float32(LOG2E))
    return np.stack(pieces + pieces, axis=1)


def _causal_stack(rows, width, period):
    r = (np.arange(rows) % period)[:, None]
    c = np.arange(width)[None, :]
    zero = np.zeros((rows, width), np.float32)
    diag = np.where(c <= r, 0.0, NEG_INF).astype(np.float32)
    full = np.full((rows, width), NEG_INF, np.float32)
    tail = np.where(c > r, 0.0, NEG_INF).astype(np.float32)
    return np.stack([zero, diag, full, tail])


def _qk(q, k):
    return lax.dot_general(q, k, (((1,), (1,)), ((), ())), preferred_element_type=F32)


def _stage_scores(q_aug, k_ref, score_ref, key_start, n_tiles, masks):
    rows = q_aug.shape[0]
    per_chunk = KV_CHUNK // LANES
    mx = jnp.full((rows, LANES), NEG_INF, F32)
    for c0 in range(0, n_tiles, per_chunk):
        c1 = min(c0 + per_chunk, n_tiles)
        s = _qk(q_aug, k_ref[key_start + c0 * LANES:key_start + c1 * LANES, :])
        for t in range(c0, c1):
            tile = s[:, (t - c0) * LANES:(t - c0 + 1) * LANES]
            if t in masks:
                tile = tile + masks[t]
            score_ref[:, t * LANES:(t + 1) * LANES] = tile
            mx = jnp.maximum(mx, tile)
    return mx


def _softmax_pv(score_ref, lane_max, v_ref, key_start, n_tiles):
    mb = jnp.broadcast_to(jnp.max(lane_max, axis=-1, keepdims=True), lane_max.shape)
    ps = [jnp.exp2(score_ref[:, t * LANES:(t + 1) * LANES] - mb).astype(BF16) for t in range(n_tiles)]
    v = v_ref[key_start:key_start + n_tiles * LANES, :]
    return jnp.dot(jnp.concatenate(ps, axis=1), v, preferred_element_type=F32)


def _in_proj_kernel(x_ref, w_ref, h_ref, hck_ref, hcv_ref):
    xb = x_ref[...].astype(BF16)
    for c0 in range(0, N_IN_PAD, PROJ_TN):
        c1 = min(c0 + PROJ_TN, N_IN_PAD)
        r = jnp.dot(xb, w_ref[:, c0:c1], preferred_element_type=F32)
        if c0 < OFF_DK or OFF_NQ <= c0 < OFF_CK:
            r = r * Q_SCALE
        h_ref[:, c0:c1] = r.astype(BF16)
        if c0 == OFF_CK:
            hck_ref[...] = r[:, :NSA_KV_WIDTH]
            hcv_ref[...] = r[:, NSA_KV_WIDTH:]


def _in_proj(x2, w_pad):
    m = x2.shape[0]
    return pl.pallas_call(
        _in_proj_kernel,
        grid=(m // PROJ_TM,),
        in_specs=[pl.BlockSpec((PROJ_TM, D_MODEL), lambda i: (i, 0)),
                  pl.BlockSpec((D_MODEL, N_IN_PAD), lambda i: (0, 0))],
        out_specs=[pl.BlockSpec((PROJ_TM, N_IN_PAD), lambda i: (i, 0)),
                   pl.BlockSpec((PROJ_TM, NSA_KV_WIDTH), lambda i: (i, 0)),
                   pl.BlockSpec((PROJ_TM, NSA_KV_WIDTH), lambda i: (i, 0))],
        out_shape=[jax.ShapeDtypeStruct((m, N_IN_PAD), BF16),
                   jax.ShapeDtypeStruct((m, NSA_KV_WIDTH), F32),
                   jax.ShapeDtypeStruct((m, NSA_KV_WIDTH), F32)],
        compiler_params=_params("parallel"),
        name="in_proj",
    )(x2, w_pad)


def _compress_kernel(hck_ref, hcv_ref, pek_ref, w1k_ref, w2k_ref, pev_ref, w1v_ref, w2v_ref,
                     kc_ref, vc_ref):
    n_rows = hck_ref.shape[0] // CMP_STRIDE
    half = CMP_BLOCK // 2
    for src, pe_ref, w1_ref, w2_ref, out_ref in ((hck_ref, pek_ref, w1k_ref, w2k_ref, kc_ref),
                                                   (hcv_ref, pev_ref, w1v_ref, w2v_ref, vc_ref)):
        acc_a = jnp.zeros((NSA_KV_HEADS * n_rows, CMP_HIDDEN), F32)
        acc_b = jnp.zeros((NSA_KV_HEADS * n_rows, CMP_HIDDEN), F32)
        for l in range(half):
            xl = src[pl.ds(l, n_rows, stride=CMP_STRIDE), :]
            xs = jnp.concatenate([xl[:, g * HEAD_DIM:(g + 1) * HEAD_DIM]
                                  for g in range(NSA_KV_HEADS)], axis=0)
            xa = (xs + pe_ref[l:l + 1, :]).astype(BF16)
            xb = (xs + pe_ref[half + l:half + l + 1, :]).astype(BF16)
            wa = w1_ref[l * HEAD_DIM:(l + 1) * HEAD_DIM, :].astype(BF16)
            wb = w1_ref[(half + l) * HEAD_DIM:(half + l + 1) * HEAD_DIM, :].astype(BF16)
            acc_a = acc_a + jnp.dot(xa, wa, preferred_element_type=F32)
            acc_b = acc_b + jnp.dot(xb, wb, preferred_element_type=F32)
        w2 = w2_ref[...].astype(BF16)
        for g in range(NSA_KV_HEADS):
            a = acc_a[g * n_rows:(g + 1) * n_rows]
            b = acc_b[g * n_rows:(g + 1) * n_rows]
            hid = a + pltpu.roll(b, n_rows - 1, 0)
            act = jax.nn.gelu(hid).astype(BF16)
            if out_ref is kc_ref:
                out = jnp.dot(act, w2, preferred_element_type=F32)
                out_ref[:, g * HEAD_DIM:(g + 1) * HEAD_DIM] = out.astype(out_ref.dtype)
            else:
                out_ref[g] = _qk(w2, act).astype(out_ref.dtype)


def _compress(hck, hcv, pe_k, w1_k, w2_k, pe_v, w1_v, w2_v, batch, seq):
    n_rows = seq // CMP_STRIDE
    w2_vt = w2_v.T
    full = lambda shape: pl.BlockSpec(shape, lambda b: (0,) * len(shape))
    kv_spec = pl.BlockSpec((seq, NSA_KV_WIDTH), lambda b: (b, 0))
    return pl.pallas_call(
        _compress_kernel,
        grid=(batch,),
        in_specs=[kv_spec, kv_spec,
                  full(pe_k.shape), full(w1_k.shape), full(w2_k.shape),
                  full(pe_v.shape), full(w1_v.shape), full(w2_vt.shape)],
        out_specs=[pl.BlockSpec((None, n_rows, NSA_KV_WIDTH), lambda b: (b, 0, 0)),
                   pl.BlockSpec((None, NSA_KV_HEADS, HEAD_DIM, n_rows), lambda b: (b, 0, 0, 0))],
        out_shape=[jax.ShapeDtypeStruct((batch, n_rows, NSA_KV_WIDTH), BF16),
                   jax.ShapeDtypeStruct((batch, NSA_KV_HEADS, HEAD_DIM, n_rows), BF16)],
        compiler_params=_params("parallel"),
        name="compress",
    )(hck, hcv, pe_k, w1_k, w2_k, pe_v, w1_v, w2_vt)


def _diff_kernel(q_ref, k_ref, v_ref, pos_ref, srow_ref, mask_ref, lq1_ref, lk1_ref, lq2_ref, lk2_ref,
                 g_ref, o_ref, kaug_ref, vaug_ref, s_ref, *, lam_init):
    step = q_ref.shape[0]
    seq = k_ref.shape[0]
    dv = 2 * HEAD_DIM
    n_heads = q_ref.shape[1] // LANES
    qi = pl.program_id(2)
    low = lax.broadcasted_iota(jnp.int32, (1, LANES), 1) < HEAD_DIM
    halves = (low, jnp.logical_not(low))
    head_lanes = [slice(hh * LANES, (hh + 1) * LANES) for hh in range(n_heads)]

    @pl.when(qi == 0)
    def _():
        for hh in range(n_heads):
            k = k_ref[:, head_lanes[hh]]
            for c in range(2):
                kaug_ref[hh, c] = jnp.where(halves[c], k, pos_ref[...])
            vaug_ref[hh, :, :dv] = v_ref[:, head_lanes[hh]]
            vaug_ref[hh, :, dv:] = jnp.ones((seq, dv), BF16)

    q_aug = []
    for hh in range(n_heads):
        q = q_ref[:, head_lanes[hh]]
        srow = jnp.broadcast_to(srow_ref[hh:hh + 1, :], q.shape)
        q_aug.append([jnp.where(halves[c], q, srow) for c in range(2)])
    lam = (jnp.exp(jnp.sum(lq1_ref[...] * lk1_ref[...], axis=-1, keepdims=True))
           - jnp.exp(jnp.sum(lq2_ref[...] * lk2_ref[...], axis=-1, keepdims=True)) + lam_init)

    tq = mask_ref.shape[1]
    per_chain = tq // LANES

    def sweep(step_idx):
        for r in range(step // tq):
            rows = slice(r * tq, (r + 1) * tq)
            n_tiles = (step_idx * step + (r + 1) * tq) // LANES
            masks = {n_tiles - per_chain + u: mask_ref[u] for u in range(per_chain)}
            for hh in range(n_heads):
                outs = []
                for c in range(2):
                    buf = s_ref.at[hh * 2 + c, r % DIFF_SCORE_BUFS]
                    lane_max = _stage_scores(q_aug[hh][c][rows], kaug_ref.at[hh, c], buf, 0, n_tiles, masks)
                    acc = _softmax_pv(buf, lane_max, vaug_ref.at[hh], 0, n_tiles)
                    outs.append(acc[:, :dv] / acc[:, dv:])
                o = outs[0] - lam * outs[1]
                o = o * lax.rsqrt(jnp.mean(o * o, axis=-1, keepdims=True) + RMS_EPS) * g_ref[...]
                o_ref[rows, head_lanes[hh]] = (o * (1.0 - lam_init)).astype(o_ref.dtype)

    for step_idx in range(seq // step):
        pl.when(qi == step_idx)(functools.partial(sweep, step_idx))


def _diff_attn(h3, lq1, lk1, lq2, lk2, subln_g, slopes):
    batch, seq, _ = h3.shape
    tq = DIFF_STEP
    assert seq % tq == 0 and tq % DIFF_TQ == 0 and DIFF_TQ % LANES == 0
    assert DIFF_HEADS % DIFF_HEADS_PER_STEP == 0 and (OFF_DK // LANES) % DIFF_HEADS_PER_STEP == 0
    kb, vb = OFF_DK // LANES, OFF_DV // LANES
    dv = 2 * HEAD_DIM

    bias = _bias_columns(seq)
    pos = np.zeros((seq, LANES), np.float32)
    srow = np.zeros((DIFF_HEADS, 1, LANES), np.float32)
    slope_cols = _slope_columns(slopes[:DIFF_HEADS])
    for base in (0, HEAD_DIM):
        pos[:, base:base + N_BIAS_COLS] = bias
        srow[:, 0, base:base + N_BIAS_COLS] = slope_cols
    masks = _causal_stack(DIFF_TQ, DIFF_TQ, DIFF_TQ)[1].reshape(DIFF_TQ, DIFF_TQ // LANES, LANES).transpose(1, 0, 2)

    hn = DIFF_HEADS_PER_STEP
    width = hn * LANES
    kb, vb = kb // hn, vb // hn
    vec = lambda n: pl.BlockSpec((1, n), lambda b, h, i: (0, 0))
    return pl.pallas_call(
        functools.partial(_diff_kernel, lam_init=_lambda_init(0)),
        grid=(batch, DIFF_HEADS // hn, seq // tq),
        in_specs=[pl.BlockSpec((None, tq, width), lambda b, h, i: (b, i, h)),
                  pl.BlockSpec((None, seq, width), lambda b, h, i: (b, 0, kb + h)),
                  pl.BlockSpec((None, seq, width), lambda b, h, i: (b, 0, vb + h)),
                  pl.BlockSpec((seq, LANES), lambda b, h, i: (0, 0)),
                  pl.BlockSpec((None, hn, LANES), lambda b, h, i: (h, 0, 0)),
                  pl.BlockSpec(masks.shape, lambda b, h, i: (0, 0, 0)),
                  vec(HEAD_DIM), vec(HEAD_DIM), vec(HEAD_DIM), vec(HEAD_DIM), vec(dv)],
        out_specs=pl.BlockSpec((None, tq, width), lambda b, h, i: (b, i, h)),
        out_shape=jax.ShapeDtypeStruct((batch, seq, DIFF_WIDTH), BF16),
        scratch_shapes=[pltpu.VMEM((hn, 2, seq, LANES), BF16),
                        pltpu.VMEM((hn, seq, 2 * dv), BF16),
                        pltpu.VMEM((hn * 2, DIFF_SCORE_BUFS, DIFF_TQ, seq), F32)],
        compiler_params=_params("parallel", "parallel", "arbitrary"),
        name="diff_attn",
    )(h3, h3, h3, jnp.asarray(pos, BF16), jnp.asarray(srow.reshape(DIFF_HEADS // hn, hn, LANES), BF16),
      jnp.asarray(masks),
      lq1, lk1, lq2, lk2, subln_g)


def _split3(x):
    hi = x.astype(BF16)
    r1 = x - hi.astype(F32)
    mid = r1.astype(BF16)
    lo = (r1 - mid.astype(F32)).astype(BF16)
    return hi, mid, lo


def _compressed_branch(q_cmp, kc_aug, vct, t0, need_rank):
    tq = q_cmp.shape[0] // NSA_GROUP
    n_rows = kc_aug.shape[0]
    n_sel = LANES // 4
    slot = lax.broadcasted_iota(jnp.int32, (n_rows, tq), 0)
    t_q = t0 + lax.broadcasted_iota(jnp.int32, (n_rows, tq), 1)
    mask_c = (slot * CMP_STRIDE + (CMP_BLOCK - 1)) <= t_q
    maskf = jnp.where(mask_c, 1.0, 0.0)

    s_t = _qk(kc_aug, q_cmp)
    ps = []
    for i in range(NSA_GROUP):
        s = jnp.where(mask_c, s_t[:, i * tq:(i + 1) * tq], NEG_INF)
        e = jnp.exp2(s - jnp.max(s, axis=0, keepdims=True))
        ps.append(e / jnp.sum(e, axis=0, keepdims=True) * maskf)
    o_t = jnp.dot(vct, jnp.concatenate(ps, axis=1).astype(BF16), preferred_element_type=F32)
    pairs = [jnp.concatenate([o_t[:, (2 * j) * tq:(2 * j + 1) * tq],
                              o_t[:, (2 * j + 1) * tq:(2 * j + 2) * tq]], axis=0).T
             for j in range(NSA_GROUP // 2)]

    blk = lax.broadcasted_iota(jnp.int32, (n_sel, tq), 0)
    cur = (t0 + lax.broadcasted_iota(jnp.int32, (n_sel, tq), 1)) // SEL_BLOCK
    future = blk > cur
    if not need_rank:
        keep = jnp.logical_not(future)
    else:
        s_i = lax.broadcasted_iota(jnp.int32, (n_sel, n_rows), 0)
        n_i = lax.broadcasted_iota(jnp.int32, (n_sel, n_rows), 1)
        ov_t = ((n_i * CMP_STRIDE <= s_i * SEL_BLOCK + (SEL_BLOCK - 1))
                & (n_i * CMP_STRIDE + (CMP_BLOCK - 1) >= s_i * SEL_BLOCK)).astype(BF16)
        psum = ps[0] + ps[1] + ps[2] + ps[3]
        imp_t = jnp.zeros((n_sel, tq), F32)
        for piece in _split3(psum):
            imp_t = imp_t + jnp.dot(ov_t, piece, preferred_element_type=F32)
        forced = (blk == 0) | (blk == cur) | (blk == cur - 1)
        val = jnp.where(forced, jnp.inf, jnp.where(future, -jnp.inf, imp_t))
        rank = jnp.zeros((n_sel, tq), F32)
        for sp in range(n_sel):
            other = val[sp:sp + 1, :]
            beats = (other > val) | ((other == val) & (sp < blk))
            rank = rank + jnp.where(beats, 1.0, 0.0)
        keep = (rank < float(SEL_TOPN)) & jnp.logical_not(future)
    return pairs, jnp.where(keep, 0.0, NEG_INF)


def _cmp_select_kernel(q_ref, kc_ref, vct_ref, ocmp_ref, selx_ref):
    tq = q_ref.shape[0]
    n_sel = LANES // 4
    qi = pl.program_id(1)
    n_plain = (SEL_TOPN * SEL_BLOCK) // tq

    def tile(need_rank):
        q = q_ref[...]
        for g in range(NSA_KV_HEADS):
            q_cmp = jnp.concatenate([q[:, i * LANES + g * HEAD_DIM:i * LANES + (g + 1) * HEAD_DIM]
                                     for i in range(NSA_GROUP)], axis=0)
            pairs, pen = _compressed_branch(q_cmp, kc_ref[:, g * HEAD_DIM:(g + 1) * HEAD_DIM], vct_ref[g],
                                            qi * tq, need_rank)
            for j in range(NSA_GROUP // 2):
                ocmp_ref[g * (NSA_GROUP // 2) + j] = pairs[j].astype(ocmp_ref.dtype)
            base = NSA_EXTRA_BASE[g]
            rows = [jnp.zeros((base, tq), F32)] if base else []
            rows += [pen, jnp.zeros((LANES - base - n_sel, tq), F32)]
            selx_ref[g] = jnp.concatenate(rows, axis=0).T.astype(selx_ref.dtype)

    pl.when(qi < n_plain)(functools.partial(tile, False))
    pl.when(qi >= n_plain)(functools.partial(tile, True))


def _cmp_select(h3, kc, vct):
    batch, seq, _ = h3.shape
    tq = CMP_TQ
    n_pairs = NSA_HEADS // 2
    assert 4 * (seq // SEL_BLOCK) == LANES and 2 * HEAD_DIM == LANES and (SEL_TOPN * SEL_BLOCK) % tq == 0
    return pl.pallas_call(
        _cmp_select_kernel,
        grid=(batch, seq // tq),
        in_specs=[pl.BlockSpec((None, tq, NSA_WIDTH), lambda b, i: (b, i, OFF_NQ // NSA_WIDTH)),
                  pl.BlockSpec((None,) + kc.shape[1:], lambda b, i: (b, 0, 0)),
                  pl.BlockSpec((None,) + vct.shape[1:], lambda b, i: (b, 0, 0, 0))],
        out_specs=[pl.BlockSpec((None, n_pairs, tq, LANES), lambda b, i: (b, 0, i, 0)),
                   pl.BlockSpec((None, NSA_KV_HEADS, tq, LANES), lambda b, i: (b, 0, i, 0))],
        out_shape=[jax.ShapeDtypeStruct((batch, n_pairs, seq, LANES), BF16),
                   jax.ShapeDtypeStruct((batch, NSA_KV_HEADS, seq, LANES), BF16)],
        compiler_params=_params("parallel", "parallel"),
        name="cmp_select",
    )(h3, kc, vct)


def _nsa_kernel(q_ref, sk_ref, sv_ref, wk_ref, wv_ref, gate_ref, selx_ref, ocmp_ref, ext_ref, srow_ref,
                mask_ref, gexp_ref, o_ref, ksel_ref, vsel_ref, kwin_ref, vwin_ref, ssel_ref, swin_ref):
    step = q_ref.shape[0]
    seq = sk_ref.shape[0]
    tq = ATT_TQ
    qi = pl.program_id(1)
    win_tiles = WINDOW // tq + 1
    lane = lax.broadcasted_iota(jnp.int32, (1, LANES), 1)
    halves = (lane < HEAD_DIM, lane >= HEAD_DIM)

    @pl.when(qi == 0)
    def _():
        for g in range(NSA_KV_HEADS):
            ksel_ref[g] = jnp.where(halves[g], sk_ref[...], ext_ref[g])
            kwin_ref[g] = jnp.where(halves[g], wk_ref[...], ext_ref[g])
            vsel_ref[g] = jnp.where(halves[g], sv_ref[...], jnp.ones((seq, LANES), BF16))
            vwin_ref[g] = jnp.where(halves[g], wv_ref[...], jnp.ones((seq, LANES), BF16))

    def group_sweep(step_idx, g, carry):
        half = (lane // HEAD_DIM) == g
        is_g0 = g == 0
        q = q_ref[...]
        pen = selx_ref[g]
        diag, tail = mask_ref[0], mask_ref[1]
        gsig = jax.nn.sigmoid(gate_ref[...].astype(F32)).astype(BF16)
        for r in range(step // tq):
            rsl = slice(r * tq, (r + 1) * tq)
            q_sel, q_win = [], []
            for i in range(NSA_GROUP):
                blk = q[rsl, i * LANES:(i + 1) * LANES]
                srow = jnp.broadcast_to(srow_ref[g * NSA_GROUP + i], blk.shape)
                q_sel.append(jnp.where(half, blk, pen[rsl] + srow))
                q_win.append(jnp.where(half, blk, srow))
            q_sel = jnp.concatenate(q_sel, axis=0)
            q_win = jnp.concatenate(q_win, axis=0)

            n_tiles = (step_idx * step + (r + 1) * tq) // LANES
            sbuf, wbuf = ssel_ref.at[r % SCORE_BUFS], swin_ref.at[r % SCORE_BUFS]
            if n_tiles < win_tiles:
                w_start, w_tiles, w_masks = 0, n_tiles, {n_tiles - 1: diag}
            else:
                w_start, w_tiles = (n_tiles - win_tiles) * LANES, win_tiles
                w_masks = {0: tail, win_tiles - 1: diag}
            max_s = _stage_scores(q_sel, ksel_ref.at[g], sbuf, 0, n_tiles, {n_tiles - 1: diag})
            max_w = _stage_scores(q_win, kwin_ref.at[g], wbuf, w_start, w_tiles, w_masks)
            acc_s = _softmax_pv(sbuf, max_s, vsel_ref.at[g], 0, n_tiles)
            acc_w = _softmax_pv(wbuf, max_w, vwin_ref.at[g], w_start, w_tiles)
            o_sel = acc_s / pltpu.roll(acc_s, HEAD_DIM, 1)
            o_win = acc_w / pltpu.roll(acc_w, HEAD_DIM, 1)
            gate = [jnp.dot(gsig[rsl], gexp_ref[g, br], preferred_element_type=F32) for br in range(3)]
            for j in range(NSA_GROUP // 2):
                first = slice((2 * j) * tq, (2 * j + 1) * tq)
                second = slice((2 * j + 1) * tq, (2 * j + 2) * tq)

                def head_pair(o):
                    keep = jnp.where(is_g0, o[first], o[second])
                    move = jnp.where(is_g0, o[second], o[first])
                    return jnp.where(half, keep, pltpu.roll(move, HEAD_DIM, 1))

                cols = slice(j * LANES, (j + 1) * LANES)
                blk_idx = g * (NSA_GROUP // 2) + j
                out = (gate[0][:, cols] * ocmp_ref[blk_idx, rsl, :].astype(F32)
                       + gate[1][:, cols] * head_pair(o_sel) + gate[2][:, cols] * head_pair(o_win))
                o_ref[blk_idx, rsl, :] = out.astype(o_ref.dtype)
        return carry

    for step_idx in range(seq // step):
        @pl.when(qi == step_idx)
        def _(step_idx=step_idx):
            lax.fori_loop(0, NSA_KV_HEADS, functools.partial(group_sweep, step_idx), 0)


def _nsa_attn(h3, selx, ocmp, slopes):
    batch, seq, _ = h3.shape
    tq = NSA_STEP
    assert seq % tq == 0 and tq % ATT_TQ == 0 and ATT_TQ == LANES and WINDOW % ATT_TQ == 0
    n_sel = seq // SEL_BLOCK
    rows = NSA_GROUP * ATT_TQ
    win_keys = WINDOW + ATT_TQ

    bias = _bias_columns(seq)
    onehot = (np.arange(seq)[:, None] // SEL_BLOCK == np.arange(n_sel)[None, :]).astype(np.float32)
    ext = np.zeros((NSA_KV_HEADS, seq, LANES), np.float32)
    srow = np.zeros((NSA_HEADS, 1, LANES), np.float32)
    slope_cols = _slope_columns(slopes[DIFF_HEADS:])
    for g in range(NSA_KV_HEADS):
        base = NSA_EXTRA_BASE[g]
        ext[g, :, base:base + n_sel] = onehot
        ext[g, :, base + n_sel:base + n_sel + N_BIAS_COLS] = bias
        srow[g * NSA_GROUP:(g + 1) * NSA_GROUP, 0, base + n_sel:base + n_sel + N_BIAS_COLS] = \
            slope_cols[g * NSA_GROUP:(g + 1) * NSA_GROUP]
    masks = _causal_stack(rows, LANES, ATT_TQ)[[1, 3]]
    gexp = np.zeros((NSA_KV_HEADS, 3, LANES, NSA_GROUP * HEAD_DIM), np.float32)
    for g in range(NSA_KV_HEADS):
        for br in range(3):
            for i in range(NSA_GROUP):
                gexp[g, br, 3 * (g * NSA_GROUP + i) + br, i * HEAD_DIM:(i + 1) * HEAD_DIM] = 1.0

    n_pairs = NSA_HEADS // 2
    kvspec = lambda off: pl.BlockSpec((None, seq, LANES), lambda b, i: (b, 0, off // LANES))
    tile = lambda w, blk: pl.BlockSpec((None, tq, w), lambda b, i: (b, i, blk))
    blocks = lambda n: pl.BlockSpec((None, n, tq, LANES), lambda b, i: (b, 0, i, 0))
    const = lambda a: pl.BlockSpec(a.shape, lambda b, i: (0,) * a.ndim)
    return pl.pallas_call(
        _nsa_kernel,
        grid=(batch, seq // tq),
        in_specs=[tile(NSA_WIDTH, OFF_NQ // NSA_WIDTH),
                  kvspec(OFF_SK), kvspec(OFF_SV), kvspec(OFF_WK), kvspec(OFF_WV),
                  tile(LANES, OFF_G // LANES), blocks(NSA_KV_HEADS), blocks(n_pairs),
                  const(ext), const(srow), const(masks), const(gexp)],
        out_specs=blocks(n_pairs),
        out_shape=jax.ShapeDtypeStruct((batch, n_pairs, seq, LANES), BF16),
        scratch_shapes=[pltpu.VMEM((NSA_KV_HEADS, seq, LANES), BF16) for _ in range(4)]
                       + [pltpu.VMEM((SCORE_BUFS, rows, seq), F32),
                          pltpu.VMEM((SCORE_BUFS, rows, win_keys), F32)],
        compiler_params=_params("parallel", "arbitrary"),
        name="nsa_attn",
    )(h3, h3, h3, h3, h3, h3, selx, ocmp, jnp.asarray(ext, BF16), jnp.asarray(srow, BF16),
      jnp.asarray(masks), jnp.asarray(gexp, BF16))


def _layer_norm(y, g, b):
    mu = jnp.mean(y, axis=-1, keepdims=True)
    yc = y - mu
    var = jnp.mean(yc * yc, axis=-1, keepdims=True)
    return yc * lax.rsqrt(var + LN_EPS) * g + b


def _out_ffn_kernel(od_ref, on_ref, x_ref, wo_ref, g1_ref, b1_ref, wg_ref, wu_ref, wd_ref,
                    g2_ref, b2_ref, o_ref):
    for r0 in range(0, x_ref.shape[0], FFN_CHAIN):
        rows = slice(r0, r0 + FFN_CHAIN)
        att = jnp.concatenate([od_ref[rows, :]] + [on_ref[j, rows, :] for j in range(on_ref.shape[0])], axis=1)
        mix = jnp.dot(att, wo_ref[...], preferred_element_type=F32)
        x1 = _layer_norm(DEEPNORM_ALPHA * x_ref[rows, :] + mix, g1_ref[...], b1_ref[...])
        x1b = x1.astype(BF16)
        acc = jnp.zeros(x1.shape, F32)
        for c0 in range(0, D_FF, FFN_TF):
            gate = jnp.dot(x1b, wg_ref[:, c0:c0 + FFN_TF], preferred_element_type=F32)
            up = jnp.dot(x1b, wu_ref[:, c0:c0 + FFN_TF], preferred_element_type=F32)
            act = (jax.nn.silu(gate) * up).astype(BF16)
            acc = acc + jnp.dot(act, wd_ref[c0:c0 + FFN_TF, :], preferred_element_type=F32)
        o_ref[rows, :] = _layer_norm(DEEPNORM_ALPHA * x1 + acc, g2_ref[...], b2_ref[...])


def _out_ffn(od, on, x2, wo, g1, b1, wg, wu, wd, g2, b2):
    m = x2.shape[0]
    tm = FFN_TM
    _, n_pairs, seq, _ = on.shape
    per_seq = seq // tm
    const = lambda shape: pl.BlockSpec(shape, lambda i: (0, 0), pipeline_mode=pl.Buffered(1))
    return pl.pallas_call(
        _out_ffn_kernel,
        grid=(m // tm,),
        in_specs=[pl.BlockSpec((tm, DIFF_WIDTH), lambda i: (i, 0)),
                  pl.BlockSpec((None, n_pairs, tm, LANES), lambda i: (i // per_seq, 0, i % per_seq, 0)),
                  pl.BlockSpec((tm, D_MODEL), lambda i: (i, 0)),
                  const(wo.shape), const(g1.shape), const(b1.shape),
                  const(wg.shape), const(wu.shape), const(wd.shape),
                  const(g2.shape), const(b2.shape)],
        out_specs=pl.BlockSpec((tm, D_MODEL), lambda i: (i, 0)),
        out_shape=jax.ShapeDtypeStruct((m, D_MODEL), F32),
        compiler_params=_params("parallel"),
        name="out_ffn",
    )(od, on, x2, wo, g1, b1, wg, wu, wd, g2, b2)


def kernel(x, w_in, diff_lq1, diff_lk1, diff_lq2, diff_lk2, diff_subln_g, cmp_pe_k, cmp_w1_k, cmp_w2_k,
           cmp_pe_v, cmp_w1_v, cmp_w2_v, w_out, ln1_g, ln1_b, w_gate, w_up, w_down, ln2_g, ln2_b):
    batch, seq, _ = x.shape
    assert w_in.shape[0] == DEPTH
    assert seq % DIFF_STEP == 0 and seq % NSA_STEP == 0
    assert (batch * seq) % PROJ_TM == 0 and (batch * seq) % FFN_TM == 0
    slopes = _alibi_slopes()
    x2 = x.reshape(batch * seq, D_MODEL)

    w = w_in[0]
    head_order = [g * NSA_GROUP + i for i in range(NSA_GROUP) for g in range(NSA_KV_HEADS)]
    w_nq = w[:, OFF_NQ:OFF_CK].reshape(D_MODEL, NSA_HEADS, HEAD_DIM)[:, head_order, :]
    w_pad = jnp.concatenate([w[:, :OFF_NQ], w_nq.reshape(D_MODEL, NSA_WIDTH), w[:, OFF_CK:],
                             jnp.zeros((D_MODEL, N_IN_PAD - N_IN), w.dtype)], axis=1).astype(BF16)
    h, hck, hcv = _in_proj(x2, w_pad)
    h3 = h.reshape(batch, seq, N_IN_PAD)

    kc, vct = _compress(hck, hcv, cmp_pe_k[0], cmp_w1_k[0], cmp_w2_k[0],
                        cmp_pe_v[0], cmp_w1_v[0], cmp_w2_v[0], batch, seq)
    o_diff = _diff_attn(h3, diff_lq1, diff_lk1, diff_lq2, diff_lk2, diff_subln_g, slopes)
    o_cmp, selx = _cmp_select(h3, kc, vct)
    o_nsa = _nsa_attn(h3, selx, o_cmp, slopes)

    assert seq % FFN_TM == 0
    out = _out_ffn(o_diff.reshape(batch * seq, DIFF_WIDTH), o_nsa, x2,
                   w_out[0].astype(BF16), ln1_g, ln1_b,
                   w_gate[0].astype(BF16), w_up[0].astype(BF16), w_down[0].astype(BF16), ln2_g, ln2_b)
    return out.reshape(batch, seq, D_MODEL)
```

```python
import functools
import math

import numpy as np
import jax
import jax.numpy as jnp
from jax import lax
from jax.experimental import pallas as pl
from jax.experimental.pallas import tpu as pltpu

F32 = jnp.float32
BF16 = jnp.bfloat16

D_MODEL = 1024
HEAD_DIM = 64
DIFF_HEADS = 4
DIFF_WIDTH = DIFF_HEADS * 2 * HEAD_DIM
NSA_HEADS = 8
NSA_KV_HEADS = 2
NSA_GROUP = NSA_HEADS // NSA_KV_HEADS
NSA_WIDTH = NSA_HEADS * HEAD_DIM
NSA_KV_WIDTH = NSA_KV_HEADS * HEAD_DIM
CMP_BLOCK = 32
CMP_STRIDE = 16
CMP_HIDDEN = 128
SEL_BLOCK = 64
SEL_TOPN = 16
WINDOW = 512
N_GATES = 3 * NSA_HEADS
D_FF = 2816
N_ALIBI = DIFF_HEADS + NSA_HEADS
LN_EPS = 1e-5
RMS_EPS = 1e-5
NEG_INF = -1e30
DEPTH = 1
DEEPNORM_ALPHA = (2.0 * DEPTH) ** 0.25
SCALE = HEAD_DIM ** -0.5
LOG2E = 1.4426950408889634
Q_SCALE = SCALE * LOG2E

OFF_DQ = 0
OFF_DK = OFF_DQ + DIFF_WIDTH
OFF_DV = OFF_DK + DIFF_WIDTH
OFF_NQ = OFF_DV + DIFF_WIDTH
OFF_CK = OFF_NQ + NSA_WIDTH
OFF_CV = OFF_CK + NSA_KV_WIDTH
OFF_SK = OFF_CV + NSA_KV_WIDTH
OFF_SV = OFF_SK + NSA_KV_WIDTH
OFF_WK = OFF_SV + NSA_KV_WIDTH
OFF_WV = OFF_WK + NSA_KV_WIDTH
OFF_G = OFF_WV + NSA_KV_WIDTH
N_IN = OFF_G + N_GATES

LANES = 128
N_IN_PAD = -(-N_IN // LANES) * LANES
VMEM_LIMIT = 56 * 1024 * 1024

PROJ_TM = 512
PROJ_TN = 256
CMP_TQ = 1024
DIFF_STEP = 2048
NSA_STEP = 1024
ATT_TQ = 128
DIFF_TQ = 128
DIFF_HEADS_PER_STEP = 4
SCORE_BUFS = 4
DIFF_SCORE_BUFS = 2
KV_CHUNK = 512
FFN_TM = 1024
FFN_CHAIN = 512
FFN_TF = 256

POS_HI_UNIT = 256
N_PIECES = 3
N_BIAS_COLS = 2 * N_PIECES
NSA_EXTRA_BASE = (HEAD_DIM, 0)


def _lambda_init(layer_idx):
    return 0.8 - 0.6 * math.exp(-0.3 * layer_idx)


def _alibi_slopes():
    return np.asarray(2.0 ** (-8.0 * (np.arange(N_ALIBI) + 1) / N_ALIBI), dtype=np.float32)


def _params(*sem):
    return pltpu.CompilerParams(dimension_semantics=sem, vmem_limit_bytes=VMEM_LIMIT)


def _bf16_pieces(x):
    rest = np.asarray(x, np.float32)
    out = []
    for _ in range(N_PIECES):
        piece = rest.astype(BF16).astype(np.float32)
        out.append(piece)
        rest = rest - piece
    return out


def _bias_columns(seq):
    t = np.arange(seq)
    hi = ((t // POS_HI_UNIT) * POS_HI_UNIT).astype(np.float32)
    lo = (t % POS_HI_UNIT).astype(np.float32)
    return np.stack([hi] * N_PIECES + [lo] * N_PIECES, axis=1)


def _slope_columns(slopes):
    pieces = _bf16_pieces(np.asarray(slopes, np.float32) * np.float32(LOG2E))
    return np.stack(pieces + pieces, axis=1)


def _causal_stack(rows, width, period):
    r = (np.arange(rows) % period)[:, None]
    c = np.arange(width)[None, :]
    zero = np.zeros((rows, width), np.float32)
    diag = np.where(c <= r, 0.0, NEG_INF).astype(np.float32)
    full = np.full((rows, width), NEG_INF, np.float32)
    tail = np.where(c > r, 0.0, NEG_INF).astype(np.float32)
    return np.stack([zero, diag, full, tail])


def _qk(q, k):
    return lax.dot_general(q, k, (((1,), (1,)), ((), ())), preferred_element_type=F32)


def _stage_scores(q_aug, k_ref, score_ref, key_start, n_tiles, masks):
    rows = q_aug.shape[0]
    per_chunk = KV_CHUNK // LANES
    mx = jnp.full((rows, LANES), NEG_INF, F32)
    for c0 in range(0, n_tiles, per_chunk):
        c1 = min(c0 + per_chunk, n_tiles)
        s = _qk(q_aug, k_ref[key_start + c0 * LANES:key_start + c1 * LANES, :])
        for t in range(c0, c1):
            tile = s[:, (t - c0) * LANES:(t - c0 + 1) * LANES]
            if t in masks:
                tile = tile + masks[t]
            score_ref[:, t * LANES:(t + 1) * LANES] = tile
            mx = jnp.maximum(mx, tile)
    return mx


def _softmax_pv(score_ref, lane_max, v_ref, key_start, n_tiles):
    mb = jnp.broadcast_to(jnp.max(lane_max, axis=-1, keepdims=True), lane_max.shape)
    ps = [jnp.exp2(score_ref[:, t * LANES:(t + 1) * LANES] - mb).astype(BF16) for t in range(n_tiles)]
    v = v_ref[key_start:key_start + n_tiles * LANES, :]
    return jnp.dot(jnp.concatenate(ps, axis=1), v, preferred_element_type=F32)


def _in_proj_kernel(x_ref, w_ref, h_ref, hck_ref, hcv_ref):
    xb = x_ref[...].astype(BF16)
    for c0 in range(0, N_IN_PAD, PROJ_TN):
        c1 = min(c0 + PROJ_TN, N_IN_PAD)
        r = jnp.dot(xb, w_ref[:, c0:c1], preferred_element_type=F32)
        if c0 < OFF_DK or OFF_NQ <= c0 < OFF_CK:
            r = r * Q_SCALE
        h_ref[:, c0:c1] = r.astype(BF16)
        if c0 == OFF_CK:
            hck_ref[...] = r[:, :NSA_KV_WIDTH]
            hcv_ref[...] = r[:, NSA_KV_WIDTH:]


def _in_proj(x2, w_pad):
    m = x2.shape[0]
    return pl.pallas_call(
        _in_proj_kernel,
        grid=(m // PROJ_TM,),
        in_specs=[pl.BlockSpec((PROJ_TM, D_MODEL), lambda i: (i, 0)),
                  pl.BlockSpec((D_MODEL, N_IN_PAD), lambda i: (0, 0))],
        out_specs=[pl.BlockSpec((PROJ_TM, N_IN_PAD), lambda i: (i, 0)),
                   pl.BlockSpec((PROJ_TM, NSA_KV_WIDTH), lambda i: (i, 0)),
                   pl.BlockSpec((PROJ_TM, NSA_KV_WIDTH), lambda i: (i, 0))],
        out_shape=[jax.ShapeDtypeStruct((m, N_IN_PAD), BF16),
                   jax.ShapeDtypeStruct((m, NSA_KV_WIDTH), F32),
                   jax.ShapeDtypeStruct((m, NSA_KV_WIDTH), F32)],
        compiler_params=_params("parallel"),
        name="in_proj",
    )(x2, w_pad)


def _compress_kernel(hck_ref, hcv_ref, pek_ref, w1k_ref, w2k_ref, pev_ref, w1v_ref, w2v_ref,
                     kc_ref, vc_ref):
    n_rows = hck_ref.shape[0] // CMP_STRIDE
    half = CMP_BLOCK // 2
    for src, pe_ref, w1_ref, w2_ref, out_ref in ((hck_ref, pek_ref, w1k_ref, w2k_ref, kc_ref),
                                                   (hcv_ref, pev_ref, w1v_ref, w2v_ref, vc_ref)):
        acc_a = jnp.zeros((NSA_KV_HEADS * n_rows, CMP_HIDDEN), F32)
        acc_b = jnp.zeros((NSA_KV_HEADS * n_rows, CMP_HIDDEN), F32)
        for l in range(half):
            xl = src[pl.ds(l, n_rows, stride=CMP_STRIDE), :]
            xs = jnp.concatenate([xl[:, g * HEAD_DIM:(g + 1) * HEAD_DIM]
                                  for g in range(NSA_KV_HEADS)], axis=0)
            xa = (xs + pe_ref[l:l + 1, :]).astype(BF16)
            xb = (xs + pe_ref[half + l:half + l + 1, :]).astype(BF16)
            wa = w1_ref[l * HEAD_DIM:(l + 1) * HEAD_DIM, :].astype(BF16)
            wb = w1_ref[(half + l) * HEAD_DIM:(half + l + 1) * HEAD_DIM, :].astype(BF16)
            acc_a = acc_a + jnp.dot(xa, wa, preferred_element_type=F32)
            acc_b = acc_b + jnp.dot(xb, wb, preferred_element_type=F32)
        w2 = w2_ref[...].astype(BF16)
        for g in range(NSA_KV_HEADS):
            a = acc_a[g * n_rows:(g + 1) * n_rows]
            b = acc_b[g * n_rows:(g + 1) * n_rows]
            hid = a + pltpu.roll(b, n_rows - 1, 0)
            act = jax.nn.gelu(hid).astype(BF16)
            if out_ref is kc_ref:
                out = jnp.dot(act, w2, preferred_element_type=F32)
                out_ref[:, g * HEAD_DIM:(g + 1) * HEAD_DIM] = out.astype(out_ref.dtype)
            else:
                out_ref[g] = _qk(w2, act).astype(out_ref.dtype)


def _compress(hck, hcv, pe_k, w1_k, w2_k, pe_v, w1_v, w2_v, batch, seq):
    n_rows = seq // CMP_STRIDE
    w2_vt = w2_v.T
    full = lambda shape: pl.BlockSpec(shape, lambda b: (0,) * len(shape))
    kv_spec = pl.BlockSpec((seq, NSA_KV_WIDTH), lambda b: (b, 0))
    return pl.pallas_call(
        _compress_kernel,
        grid=(batch,),
        in_specs=[kv_spec, kv_spec,
                  full(pe_k.shape), full(w1_k.shape), full(w2_k.shape),
                  full(pe_v.shape), full(w1_v.shape), full(w2_vt.shape)],
        out_specs=[pl.BlockSpec((None, n_rows, NSA_KV_WIDTH), lambda b: (b, 0, 0)),
                   pl.BlockSpec((None, NSA_KV_HEADS, HEAD_DIM, n_rows), lambda b: (b, 0, 0, 0))],
        out_shape=[jax.ShapeDtypeStruct((batch, n_rows, NSA_KV_WIDTH), BF16),
                   jax.ShapeDtypeStruct((batch, NSA_KV_HEADS, HEAD_DIM, n_rows), BF16)],
        compiler_params=_params("parallel"),
        name="compress",
    )(hck, hcv, pe_k, w1_k, w2_k, pe_v, w1_v, w2_vt)


def _diff_kernel(q_ref, k_ref, v_ref, pos_ref, srow_ref, mask_ref, lq1_ref, lk1_ref, lq2_ref, lk2_ref,
                 g_ref, o_ref, kaug_ref, vaug_ref, s_ref, *, lam_init):
    step = q_ref.shape[0]
    seq = k_ref.shape[0]
    dv = 2 * HEAD_DIM
    n_heads = q_ref.shape[1] // LANES
    qi = pl.program_id(2)
    low = lax.broadcasted_iota(jnp.int32, (1, LANES), 1) < HEAD_DIM
    halves = (low, jnp.logical_not(low))
    head_lanes = [slice(hh * LANES, (hh + 1) * LANES) for hh in range(n_heads)]

    @pl.when(qi == 0)
    def _():
        for hh in range(n_heads):
            k = k_ref[:, head_lanes[hh]]
            for c in range(2):
                kaug_ref[hh, c] = jnp.where(halves[c], k, pos_ref[...])
            vaug_ref[hh, :, :dv] = v_ref[:, head_lanes[hh]]
            vaug_ref[hh, :, dv:] = jnp.ones((seq, dv), BF16)

    q_aug = []
    for hh in range(n_heads):
        q = q_ref[:, head_lanes[hh]]
        srow = jnp.broadcast_to(srow_ref[hh:hh + 1, :], q.shape)
        q_aug.append([jnp.where(halves[c], q, srow) for c in range(2)])
    lam = (jnp.exp(jnp.sum(lq1_ref[...] * lk1_ref[...], axis=-1, keepdims=True))
           - jnp.exp(jnp.sum(lq2_ref[...] * lk2_ref[...], axis=-1, keepdims=True)) + lam_init)

    tq = mask_ref.shape[1]
    per_chain = tq // LANES

    def sweep(step_idx):
        for r in range(step // tq):
            rows = slice(r * tq, (r + 1) * tq)
            n_tiles = (step_idx * step + (r + 1) * tq) // LANES
            masks = {n_tiles - per_chain + u: mask_ref[u] for u in range(per_chain)}
            for hh in range(n_heads):
                outs = []
                for c in range(2):
                    buf = s_ref.at[hh * 2 + c, r % DIFF_SCORE_BUFS]
                    lane_max = _stage_scores(q_aug[hh][c][rows], kaug_ref.at[hh, c], buf, 0, n_tiles, masks)
                    acc = _softmax_pv(buf, lane_max, vaug_ref.at[hh], 0, n_tiles)
                    outs.append(acc[:, :dv] / acc[:, dv:])
                o = outs[0] - lam * outs[1]
                o = o * lax.rsqrt(jnp.mean(o * o, axis=-1, keepdims=True) + RMS_EPS) * g_ref[...]
                o_ref[rows, head_lanes[hh]] = (o * (1.0 - lam_init)).astype(o_ref.dtype)

    for step_idx in range(seq // step):
        pl.when(qi == step_idx)(functools.partial(sweep, step_idx))


def _diff_attn(h3, lq1, lk1, lq2, lk2, subln_g, slopes):
    batch, seq, _ = h3.shape
    tq = DIFF_STEP
    assert seq % tq == 0 and tq % DIFF_TQ == 0 and DIFF_TQ % LANES == 0
    assert DIFF_HEADS % DIFF_HEADS_PER_STEP == 0 and (OFF_DK // LANES) % DIFF_HEADS_PER_STEP == 0
    kb, vb = OFF_DK // LANES, OFF_DV // LANES
    dv = 2 * HEAD_DIM

    bias = _bias_columns(seq)
    pos = np.zeros((seq, LANES), np.float32)
    srow = np.zeros((DIFF_HEADS, 1, LANES), np.float32)
    slope_cols = _slope_columns(slopes[:DIFF_HEADS])
    for base in (0, HEAD_DIM):
        pos[:, base:base + N_BIAS_COLS] = bias
        srow[:, 0, base:base + N_BIAS_COLS] = slope_cols
    masks = _causal_stack(DIFF_TQ, DIFF_TQ, DIFF_TQ)[1].reshape(DIFF_TQ, DIFF_TQ // LANES, LANES).transpose(1, 0, 2)

    hn = DIFF_HEADS_PER_STEP
    width = hn * LANES
    kb, vb = kb // hn, vb // hn
    vec = lambda n: pl.BlockSpec((1, n), lambda b, h, i: (0, 0))
    return pl.pallas_call(
        functools.partial(_diff_kernel, lam_init=_lambda_init(0)),
        grid=(batch, DIFF_HEADS // hn, seq // tq),
        in_specs=[pl.BlockSpec((None, tq, width), lambda b, h, i: (b, i, h)),
                  pl.BlockSpec((None, seq, width), lambda b, h, i: (b, 0, kb + h)),
                  pl.BlockSpec((None, seq, width), lambda b, h, i: (b, 0, vb + h)),
                  pl.BlockSpec((seq, LANES), lambda b, h, i: (0, 0)),
                  pl.BlockSpec((None, hn, LANES), lambda b, h, i: (h, 0, 0)),
                  pl.BlockSpec(masks.shape, lambda b, h, i: (0, 0, 0)),
                  vec(HEAD_DIM), vec(HEAD_DIM), vec(HEAD_DIM), vec(HEAD_DIM), vec(dv)],
        out_specs=pl.BlockSpec((None, tq, width), lambda b, h, i: (b, i, h)),
        out_shape=jax.ShapeDtypeStruct((batch, seq, DIFF_WIDTH), BF16),
        scratch_shapes=[pltpu.VMEM((hn, 2, seq, LANES), BF16),
                        pltpu.VMEM((hn, seq, 2 * dv), BF16),
                        pltpu.VMEM((hn * 2, DIFF_SCORE_BUFS, DIFF_TQ, seq), F32)],
        compiler_params=_params("parallel", "parallel", "arbitrary"),
        name="diff_attn",
    )(h3, h3, h3, jnp.asarray(pos, BF16), jnp.asarray(srow.reshape(DIFF_HEADS // hn, hn, LANES), BF16),
      jnp.asarray(masks),
      lq1, lk1, lq2, lk2, subln_g)


def _split3(x):
    hi = x.astype(BF16)
    r1 = x - hi.astype(F32)
    mid = r1.astype(BF16)
    lo = (r1 - mid.astype(F32)).astype(BF16)
    return hi, mid, lo


def _compressed_branch(q_cmp, kc_aug, vct, t0, need_rank):
    tq = q_cmp.shape[0] // NSA_GROUP
    n_rows = kc_aug.shape[0]
    n_sel = LANES // 4
    slot = lax.broadcasted_iota(jnp.int32, (n_rows, tq), 0)
    t_q = t0 + lax.broadcasted_iota(jnp.int32, (n_rows, tq), 1)
    mask_c = (slot * CMP_STRIDE + (CMP_BLOCK - 1)) <= t_q
    maskf = jnp.where(mask_c, 1.0, 0.0)

    s_t = _qk(kc_aug, q_cmp)
    ps = []
    for i in range(NSA_GROUP):
        s = jnp.where(mask_c, s_t[:, i * tq:(i + 1) * tq], NEG_INF)
        e = jnp.exp2(s - jnp.max(s, axis=0, keepdims=True))
        ps.append(e / jnp.sum(e, axis=0, keepdims=True) * maskf)
    o_t = jnp.dot(vct, jnp.concatenate(ps, axis=1).astype(BF16), preferred_element_type=F32)
    pairs = [jnp.concatenate([o_t[:, (2 * j) * tq:(2 * j + 1) * tq],
                              o_t[:, (2 * j + 1) * tq:(2 * j + 2) * tq]], axis=0).T
             for j in range(NSA_GROUP // 2)]

    blk = lax.broadcasted_iota(jnp.int32, (n_sel, tq), 0)
    cur = (t0 + lax.broadcasted_iota(jnp.int32, (n_sel, tq), 1)) // SEL_BLOCK
    future = blk > cur
    if not need_rank:
        keep = jnp.logical_not(future)
    else:
        s_i = lax.broadcasted_iota(jnp.int32, (n_sel, n_rows), 0)
        n_i = lax.broadcasted_iota(jnp.int32, (n_sel, n_rows), 1)
        ov_t = ((n_i * CMP_STRIDE <= s_i * SEL_BLOCK + (SEL_BLOCK - 1))
                & (n_i * CMP_STRIDE + (CMP_BLOCK - 1) >= s_i * SEL_BLOCK)).astype(BF16)
        psum = ps[0] + ps[1] + ps[2] + ps[3]
        imp_t = jnp.zeros((n_sel, tq), F32)
        for piece in _split3(psum):
            imp_t = imp_t + jnp.dot(ov_t, piece, preferred_element_type=F32)
        forced = (blk == 0) | (blk == cur) | (blk == cur - 1)
        val = jnp.where(forced, jnp.inf, jnp.where(future, -jnp.inf, imp_t))
        rank = jnp.zeros((n_sel, tq), F32)
        for sp in range(n_sel):
            other = val[sp:sp + 1, :]
            beats = (other > val) | ((other == val) & (sp < blk))
            rank = rank + jnp.where(beats, 1.0, 0.0)
        keep = (rank < float(SEL_TOPN)) & jnp.logical_not(future)
    return pairs, jnp.where(keep, 0.0, NEG_INF)


def _cmp_select_kernel(q_ref, kc_ref, vct_ref, ocmp_ref, selx_ref):
    tq = q_ref.shape[0]
    n_sel = LANES // 4
    qi = pl.program_id(1)
    n_plain = (SEL_TOPN * SEL_BLOCK) // tq

    def tile(need_rank):
        q = q_ref[...]
        for g in range(NSA_KV_HEADS):
            q_cmp = jnp.concatenate([q[:, i * LANES + g * HEAD_DIM:i * LANES + (g + 1) * HEAD_DIM]
                                     for i in range(NSA_GROUP)], axis=0)
            pairs, pen = _compressed_branch(q_cmp, kc_ref[:, g * HEAD_DIM:(g + 1) * HEAD_DIM], vct_ref[g],
                                            qi * tq, need_rank)
            for j in range(NSA_GROUP // 2):
                ocmp_ref[g * (NSA_GROUP // 2) + j] = pairs[j].astype(ocmp_ref.dtype)
            base = NSA_EXTRA_BASE[g]
            rows = [jnp.zeros((base, tq), F32)] if base else []
            rows += [pen, jnp.zeros((LANES - base - n_sel, tq), F32)]
            selx_ref[g] = jnp.concatenate(rows, axis=0).T.astype(selx_ref.dtype)

    pl.when(qi < n_plain)(functools.partial(tile, False))
    pl.when(qi >= n_plain)(functools.partial(tile, True))


def _cmp_select(h3, kc, vct):
    batch, seq, _ = h3.shape
    tq = CMP_TQ
    n_pairs = NSA_HEADS // 2
    assert 4 * (seq // SEL_BLOCK) == LANES and 2 * HEAD_DIM == LANES and (SEL_TOPN * SEL_BLOCK) % tq == 0
    return pl.pallas_call(
        _cmp_select_kernel,
        grid=(batch, seq // tq),
        in_specs=[pl.BlockSpec((None, tq, NSA_WIDTH), lambda b, i: (b, i, OFF_NQ // NSA_WIDTH)),
                  pl.BlockSpec((None,) + kc.shape[1:], lambda b, i: (b, 0, 0)),
                  pl.BlockSpec((None,) + vct.shape[1:], lambda b, i: (b, 0, 0, 0))],
        out_specs=[pl.BlockSpec((None, n_pairs, tq, LANES), lambda b, i: (b, 0, i, 0)),
                   pl.BlockSpec((None, NSA_KV_HEADS, tq, LANES), lambda b, i: (b, 0, i, 0))],
        out_shape=[jax.ShapeDtypeStruct((batch, n_pairs, seq, LANES), BF16),
                   jax.ShapeDtypeStruct((batch, NSA_KV_HEADS, seq, LANES), BF16)],
        compiler_params=_params("parallel", "parallel"),
        name="cmp_select",
    )(h3, kc, vct)


def _nsa_kernel(q_ref, sk_ref, sv_ref, wk_ref, wv_ref, gate_ref, selx_ref, ocmp_ref, ext_ref, srow_ref,
                mask_ref, gexp_ref, o_ref, ksel_ref, vsel_ref, kwin_ref, vwin_ref, ssel_ref, swin_ref):
    step = q_ref.shape[0]
    seq = sk_ref.shape[0]
    tq = ATT_TQ
    qi = pl.program_id(1)
    win_tiles = WINDOW // tq + 1
    lane = lax.broadcasted_iota(jnp.int32, (1, LANES), 1)
    halves = (lane < HEAD_DIM, lane >= HEAD_DIM)

    @pl.when(qi == 0)
    def _():
        for g in range(NSA_KV_HEADS):
            ksel_ref[g] = jnp.where(halves[g], sk_ref[...], ext_ref[g])
            kwin_ref[g] = jnp.where(halves[g], wk_ref[...], ext_ref[g])
            vsel_ref[g] = jnp.where(halves[g], sv_ref[...], jnp.ones((seq, LANES), BF16))
            vwin_ref[g] = jnp.where(halves[g], wv_ref[...], jnp.ones((seq, LANES), BF16))

    def group_sweep(step_idx, g, carry):
        half = (lane // HEAD_DIM) == g
        is_g0 = g == 0
        q = q_ref[...]
        pen = selx_ref[g]
        diag, tail = mask_ref[0], mask_ref[1]
        gsig = jax.nn.sigmoid(gate_ref[...].astype(F32)).astype(BF16)
        for r in range(step // tq):
            rsl = slice(r * tq, (r + 1) * tq)
            q_sel, q_win = [], []
            for i in range(NSA_GROUP):
                blk = q[rsl, i * LANES:(i + 1) * LANES]
                srow = jnp.broadcast_to(srow_ref[g * NSA_GROUP + i], blk.shape)
                q_sel.append(jnp.where(half, blk, pen[rsl] + srow))
                q_win.append(jnp.where(half, blk, srow))
            q_sel = jnp.concatenate(q_sel, axis=0)
            q_win = jnp.concatenate(q_win, axis=0)

            n_tiles = (step_idx * step + (r + 1) * tq) // LANES
            sbuf, wbuf = ssel_ref.at[r % SCORE_BUFS], swin_ref.at[r % SCORE_BUFS]
            if n_tiles < win_tiles:
                w_start, w_tiles, w_masks = 0, n_tiles, {n_tiles - 1: diag}
            else:
                w_start, w_tiles = (n_tiles - win_tiles) * LANES, win_tiles
                w_masks = {0: tail, win_tiles - 1: diag}
            max_s = _stage_scores(q_sel, ksel_ref.at[g], sbuf, 0, n_tiles, {n_tiles - 1: diag})
            max_w = _stage_scores(q_win, kwin_ref.at[g], wbuf, w_start, w_tiles, w_masks)
            acc_s = _softmax_pv(sbuf, max_s, vsel_ref.at[g], 0, n_tiles)
            acc_w = _softmax_pv(wbuf, max_w, vwin_ref.at[g], w_start, w_tiles)
            o_sel = acc_s / pltpu.roll(acc_s, HEAD_DIM, 1)
            o_win = acc_w / pltpu.roll(acc_w, HEAD_DIM, 1)
            gate = [jnp.dot(gsig[rsl], gexp_ref[g, br], preferred_element_type=F32) for br in range(3)]
            for j in range(NSA_GROUP // 2):
                first = slice((2 * j) * tq, (2 * j + 1) * tq)
                second = slice((2 * j + 1) * tq, (2 * j + 2) * tq)

                def head_pair(o):
                    keep = jnp.where(is_g0, o[first], o[second])
                    move = jnp.where(is_g0, o[second], o[first])
                    return jnp.where(half, keep, pltpu.roll(move, HEAD_DIM, 1))

                cols = slice(j * LANES, (j + 1) * LANES)
                blk_idx = g * (NSA_GROUP // 2) + j
                out = (gate[0][:, cols] * ocmp_ref[blk_idx, rsl, :].astype(F32)
                       + gate[1][:, cols] * head_pair(o_sel) + gate[2][:, cols] * head_pair(o_win))
                o_ref[blk_idx, rsl, :] = out.astype(o_ref.dtype)
        return carry

    for step_idx in range(seq // step):
        @pl.when(qi == step_idx)
        def _(step_idx=step_idx):
            lax.fori_loop(0, NSA_KV_HEADS, functools.partial(group_sweep, step_idx), 0)


def _nsa_attn(h3, selx, ocmp, slopes):
    batch, seq, _ = h3.shape
    tq = NSA_STEP
    assert seq % tq == 0 and tq % ATT_TQ == 0 and ATT_TQ == LANES and WINDOW % ATT_TQ == 0
    n_sel = seq // SEL_BLOCK
    rows = NSA_GROUP * ATT_TQ
    win_keys = WINDOW + ATT_TQ

    bias = _bias_columns(seq)
    onehot = (np.arange(seq)[:, None] // SEL_BLOCK == np.arange(n_sel)[None, :]).astype(np.float32)
    ext = np.zeros((NSA_KV_HEADS, seq, LANES), np.float32)
    srow = np.zeros((NSA_HEADS, 1, LANES), np.float32)
    slope_cols = _slope_columns(slopes[DIFF_HEADS:])
    for g in range(NSA_KV_HEADS):
        base = NSA_EXTRA_BASE[g]
        ext[g, :, base:base + n_sel] = onehot
        ext[g, :, base + n_sel:base + n_sel + N_BIAS_COLS] = bias
        srow[g * NSA_GROUP:(g + 1) * NSA_GROUP, 0, base + n_sel:base + n_sel + N_BIAS_COLS] = \
            slope_cols[g * NSA_GROUP:(g + 1) * NSA_GROUP]
    masks = _causal_stack(rows, LANES, ATT_TQ)[[1, 3]]
    gexp = np.zeros((NSA_KV_HEADS, 3, LANES, NSA_GROUP * HEAD_DIM), np.float32)
    for g in range(NSA_KV_HEADS):
        for br in range(3):
            for i in range(NSA_GROUP):
                gexp[g, br, 3 * (g * NSA_GROUP + i) + br, i * HEAD_DIM:(i + 1) * HEAD_DIM] = 1.0

    n_pairs = NSA_HEADS // 2
    kvspec = lambda off: pl.BlockSpec((None, seq, LANES), lambda b, i: (b, 0, off // LANES))
    tile = lambda w, blk: pl.BlockSpec((None, tq, w), lambda b, i: (b, i, blk))
    blocks = lambda n: pl.BlockSpec((None, n, tq, LANES), lambda b, i: (b, 0, i, 0))
    const = lambda a: pl.BlockSpec(a.shape, lambda b, i: (0,) * a.ndim)
    return pl.pallas_call(
        _nsa_kernel,
        grid=(batch, seq // tq),
        in_specs=[tile(NSA_WIDTH, OFF_NQ // NSA_WIDTH),
                  kvspec(OFF_SK), kvspec(OFF_SV), kvspec(OFF_WK), kvspec(OFF_WV),
                  tile(LANES, OFF_G // LANES), blocks(NSA_KV_HEADS), blocks(n_pairs),
                  const(ext), const(srow), const(masks), const(gexp)],
        out_specs=blocks(n_pairs),
        out_shape=jax.ShapeDtypeStruct((batch, n_pairs, seq, LANES), BF16),
        scratch_shapes=[pltpu.VMEM((NSA_KV_HEADS, seq, LANES), BF16) for _ in range(4)]
                       + [pltpu.VMEM((SCORE_BUFS, rows, seq), F32),
                          pltpu.VMEM((SCORE_BUFS, rows, win_keys), F32)],
        compiler_params=_params("parallel", "arbitrary"),
        name="nsa_attn",
    )(h3, h3, h3, h3, h3, h3, selx, ocmp, jnp.asarray(ext, BF16), jnp.asarray(srow, BF16),
      jnp.asarray(masks), jnp.asarray(gexp, BF16))


def _layer_norm(y, g, b):
    mu = jnp.mean(y, axis=-1, keepdims=True)
    yc = y - mu
    var = jnp.mean(yc * yc, axis=-1, keepdims=True)
    return yc * lax.rsqrt(var + LN_EPS) * g + b


def _out_ffn_kernel(od_ref, on_ref, x_ref, wo_ref, g1_ref, b1_ref, wg_ref, wu_ref, wd_ref,
                    g2_ref, b2_ref, o_ref):
    for r0 in range(0, x_ref.shape[0], FFN_CHAIN):
        rows = slice(r0, r0 + FFN_CHAIN)
        att = jnp.concatenate([od_ref[rows, :]] + [on_ref[j, rows, :] for j in range(on_ref.shape[0])], axis=1)
        mix = jnp.dot(att, wo_ref[...], preferred_element_type=F32)
        x1 = _layer_norm(DEEPNORM_ALPHA * x_ref[rows, :] + mix, g1_ref[...], b1_ref[...])
        x1b = x1.astype(BF16)
        acc = jnp.zeros(x1.shape, F32)
        for c0 in range(0, D_FF, FFN_TF):
            gate = jnp.dot(x1b, wg_ref[:, c0:c0 + FFN_TF], preferred_element_type=F32)
            up = jnp.dot(x1b, wu_ref[:, c0:c0 + FFN_TF], preferred_element_type=F32)
            act = (jax.nn.silu(gate) * up).astype(BF16)
            acc = acc + jnp.dot(act, wd_ref[c0:c0 + FFN_TF, :], preferred_element_type=F32)
        o_ref[rows, :] = _layer_norm(DEEPNORM_ALPHA * x1 + acc, g2_ref[...], b2_ref[...])


def _out_ffn(od, on, x2, wo, g1, b1, wg, wu, wd, g2, b2):
    m = x2.shape[0]
    tm = FFN_TM
    _, n_pairs, seq, _ = on.shape
    per_seq = seq // tm
    const = lambda shape: pl.BlockSpec(shape, lambda i: (0, 0), pipeline_mode=pl.Buffered(1))
    return pl.pallas_call(
        _out_ffn_kernel,
        grid=(m // tm,),
        in_specs=[pl.BlockSpec((tm, DIFF_WIDTH), lambda i: (i, 0)),
                  pl.BlockSpec((None, n_pairs, tm, LANES), lambda i: (i // per_seq, 0, i % per_seq, 0)),
                  pl.BlockSpec((tm, D_MODEL), lambda i: (i, 0)),
                  const(wo.shape), const(g1.shape), const(b1.shape),
                  const(wg.shape), const(wu.shape), const(wd.shape),
                  const(g2.shape), const(b2.shape)],
        out_specs=pl.BlockSpec((tm, D_MODEL), lambda i: (i, 0)),
        out_shape=jax.ShapeDtypeStruct((m, D_MODEL), F32),
        compiler_params=_params("parallel"),
        name="out_ffn",
    )(od, on, x2, wo, g1, b1, wg, wu, wd, g2, b2)


def kernel(x, w_in, diff_lq1, diff_lk1, diff_lq2, diff_lk2, diff_subln_g, cmp_pe_k, cmp_w1_k, cmp_w2_k,
           cmp_pe_v, cmp_w1_v, cmp_w2_v, w_out, ln1_g, ln1_b, w_gate, w_up, w_down, ln2_g, ln2_b):
    batch, seq, _ = x.shape
    assert w_in.shape[0] == DEPTH
    assert seq % DIFF_STEP == 0 and seq % NSA_STEP == 0
    assert (batch * seq) % PROJ_TM == 0 and (batch * seq) % FFN_TM == 0
    slopes = _alibi_slopes()
    x2 = x.reshape(batch * seq, D_MODEL)

    w = w_in[0]
    head_order = [g * NSA_GROUP + i for i in range(NSA_GROUP) for g in range(NSA_KV_HEADS)]
    w_nq = w[:, OFF_NQ:OFF_CK].reshape(D_MODEL, NSA_HEADS, HEAD_DIM)[:, head_order, :]
    w_pad = jnp.concatenate([w[:, :OFF_NQ], w_nq.reshape(D_MODEL, NSA_WIDTH), w[:, OFF_CK:],
                             jnp.zeros((D_MODEL, N_IN_PAD - N_IN), w.dtype)], axis=1).astype(BF16)
    h, hck, hcv = _in_proj(x2, w_pad)
    h3 = h.reshape(batch, seq, N_IN_PAD)

    kc, vct = _compress(hck, hcv, cmp_pe_k[0], cmp_w1_k[0], cmp_w2_k[0],
                        cmp_pe_v[0], cmp_w1_v[0], cmp_w2_v[0], batch, seq)
    o_diff = _diff_attn(h3, diff_lq1, diff_lk1, diff_lq2, diff_lk2, diff_subln_g, slopes)
    o_cmp, selx = _cmp_select(h3, kc, vct)
    o_nsa = _nsa_attn(h3, selx, o_cmp, slopes)

    assert seq % FFN_TM == 0
    out = _out_ffn(o_diff.reshape(batch * seq, DIFF_WIDTH), o_nsa, x2,
                   w_out[0].astype(BF16), ln1_g, ln1_b,
                   w_gate[0].astype(BF16), w_up[0].astype(BF16), w_down[0].astype(BF16), ln2_g, ln2_b)
    return out.reshape(batch, seq, D_MODEL)
```

```python
import functools
import math

import numpy as np
import jax
import jax.numpy as jnp
from jax import lax
from jax.experimental import pallas as pl
from jax.experimental.pallas import tpu as pltpu

F32 = jnp.float32
BF16 = jnp.bfloat16

D_MODEL = 1024
HEAD_DIM = 64
DIFF_HEADS = 4
DIFF_WIDTH = DIFF_HEADS * 2 * HEAD_DIM
NSA_HEADS = 8
NSA_KV_HEADS = 2
NSA_GROUP = NSA_HEADS // NSA_KV_HEADS
NSA_WIDTH = NSA_HEADS * HEAD_DIM
NSA_KV_WIDTH = NSA_KV_HEADS * HEAD_DIM
CMP_BLOCK = 32
CMP_STRIDE = 16
CMP_HIDDEN = 128
SEL_BLOCK = 64
SEL_TOPN = 16
WINDOW = 512
N_GATES = 3 * NSA_HEADS
D_FF = 2816
N_ALIBI = DIFF_HEADS + NSA_HEADS
LN_EPS = 1e-5
RMS_EPS = 1e-5
NEG_INF = -1e30
DEPTH = 1
DEEPNORM_ALPHA = (2.0 * DEPTH) ** 0.25
SCALE = HEAD_DIM ** -0.5
LOG2E = 1.4426950408889634
Q_SCALE = SCALE * LOG2E

OFF_DQ = 0
OFF_DK = OFF_DQ + DIFF_WIDTH
OFF_DV = OFF_DK + DIFF_WIDTH
OFF_NQ = OFF_DV + DIFF_WIDTH
OFF_CK = OFF_NQ + NSA_WIDTH
OFF_CV = OFF_CK + NSA_KV_WIDTH
OFF_SK = OFF_CV + NSA_KV_WIDTH
OFF_SV = OFF_SK + NSA_KV_WIDTH
OFF_WK = OFF_SV + NSA_KV_WIDTH
OFF_WV = OFF_WK + NSA_KV_WIDTH
OFF_G = OFF_WV + NSA_KV_WIDTH
N_IN = OFF_G + N_GATES

LANES = 128
N_IN_PAD = -(-N_IN // LANES) * LANES
VMEM_LIMIT = 56 * 1024 * 1024

PROJ_TM = 512
PROJ_TN = 256
CMP_TQ = 1024
DIFF_STEP = 2048
NSA_STEP = 1024
ATT_TQ = 128
DIFF_TQ = 128
DIFF_HEADS_PER_STEP = 4
SCORE_BUFS = 4
DIFF_SCORE_BUFS = 2
KV_CHUNK = 512
FFN_TM = 1024
FFN_CHAIN = 512
FFN_TF = 256

POS_HI_UNIT = 256
N_PIECES = 3
N_BIAS_COLS = 2 * N_PIECES
NSA_EXTRA_BASE = (HEAD_DIM, 0)


def _lambda_init(layer_idx):
    return 0.8 - 0.6 * math.exp(-0.3 * layer_idx)


def _alibi_slopes():
    return np.asarray(2.0 ** (-8.0 * (np.arange(N_ALIBI) + 1) / N_ALIBI), dtype=np.float32)


def _params(*sem, fuse_inputs=None):
    return pltpu.CompilerParams(dimension_semantics=sem, vmem_limit_bytes=VMEM_LIMIT,
                                allow_input_fusion=fuse_inputs)


def _bf16_pieces(x):
    rest = np.asarray(x, np.float32)
    out = []
    for _ in range(N_PIECES):
        piece = rest.astype(BF16).astype(np.float32)
        out.append(piece)
        rest = rest - piece
    return out


def _bias_columns(seq):
    t = np.arange(seq)
    hi = ((t // POS_HI_UNIT) * POS_HI_UNIT).astype(np.float32)
    lo = (t % POS_HI_UNIT).astype(np.float32)
    return np.stack([hi] * N_PIECES + [lo] * N_PIECES, axis=1)


def _slope_columns(slopes):
    pieces = _bf16_pieces(np.asarray(slopes, np.float32) * np.float32(LOG2E))
    return np.stack(pieces + pieces, axis=1)


def _causal_stack(rows, width, period):
    r = (np.arange(rows) % period)[:, None]
    c = np.arange(width)[None, :]
    zero = np.zeros((rows, width), np.float32)
    diag = np.where(c <= r, 0.0, NEG_INF).astype(np.float32)
    full = np.full((rows, width), NEG_INF, np.float32)
    tail = np.where(c > r, 0.0, NEG_INF).astype(np.float32)
    return np.stack([zero, diag, full, tail])


def _qk(q, k):
    return lax.dot_general(q, k, (((1,), (1,)), ((), ())), preferred_element_type=F32)


def _stage_scores(q_aug, k_ref, score_ref, key_start, n_tiles, masks):
    rows = q_aug.shape[0]
    per_chunk = KV_CHUNK // LANES
    mx = jnp.full((rows, LANES), NEG_INF, F32)
    for c0 in range(0, n_tiles, per_chunk):
        c1 = min(c0 + per_chunk, n_tiles)
        s = _qk(q_aug, k_ref[key_start + c0 * LANES:key_start + c1 * LANES, :])
        for t in range(c0, c1):
            tile = s[:, (t - c0) * LANES:(t - c0 + 1) * LANES]
            if t in masks:
                tile = tile + masks[t]
            score_ref[:, t * LANES:(t + 1) * LANES] = tile
            mx = jnp.maximum(mx, tile)
    return mx


def _softmax_pv(score_ref, lane_max, v_ref, key_start, n_tiles):
    mb = jnp.broadcast_to(jnp.max(lane_max, axis=-1, keepdims=True), lane_max.shape)
    ps = [jnp.exp2(score_ref[:, t * LANES:(t + 1) * LANES] - mb).astype(BF16) for t in range(n_tiles)]
    v = v_ref[key_start:key_start + n_tiles * LANES, :]
    return jnp.dot(jnp.concatenate(ps, axis=1), v, preferred_element_type=F32)


def _in_proj_kernel(x_ref, w_ref, h_ref, hck_ref, hcv_ref):
    xb = x_ref[...].astype(BF16)
    for c0 in range(0, N_IN_PAD, PROJ_TN):
        c1 = min(c0 + PROJ_TN, N_IN_PAD)
        r = jnp.dot(xb, w_ref[:, c0:c1], preferred_element_type=F32)
        if c0 < OFF_DK or OFF_NQ <= c0 < OFF_CK:
            r = r * Q_SCALE
        h_ref[:, c0:c1] = r.astype(BF16)
        if c0 == OFF_CK:
            hck_ref[...] = r[:, :NSA_KV_WIDTH]
            hcv_ref[...] = r[:, NSA_KV_WIDTH:]


def _in_proj(x2, w_pad):
    m = x2.shape[0]
    return pl.pallas_call(
        _in_proj_kernel,
        grid=(m // PROJ_TM,),
        in_specs=[pl.BlockSpec((PROJ_TM, D_MODEL), lambda i: (i, 0)),
                  pl.BlockSpec((D_MODEL, N_IN_PAD), lambda i: (0, 0))],
        out_specs=[pl.BlockSpec((PROJ_TM, N_IN_PAD), lambda i: (i, 0)),
                   pl.BlockSpec((PROJ_TM, NSA_KV_WIDTH), lambda i: (i, 0)),
                   pl.BlockSpec((PROJ_TM, NSA_KV_WIDTH), lambda i: (i, 0))],
        out_shape=[jax.ShapeDtypeStruct((m, N_IN_PAD), BF16),
                   jax.ShapeDtypeStruct((m, NSA_KV_WIDTH), F32),
                   jax.ShapeDtypeStruct((m, NSA_KV_WIDTH), F32)],
        compiler_params=_params("parallel", fuse_inputs=[False, True]),
        name="in_proj",
    )(x2, w_pad)


def _compress_kernel(hck_ref, hcv_ref, pek_ref, w1k_ref, w2k_ref, pev_ref, w1v_ref, w2v_ref,
                     kc_ref, vc_ref):
    n_rows = hck_ref.shape[0] // CMP_STRIDE
    half = CMP_BLOCK // 2
    for src, pe_ref, w1_ref, w2_ref, out_ref in ((hck_ref, pek_ref, w1k_ref, w2k_ref, kc_ref),
                                                   (hcv_ref, pev_ref, w1v_ref, w2v_ref, vc_ref)):
        acc_a = jnp.zeros((NSA_KV_HEADS * n_rows, CMP_HIDDEN), F32)
        acc_b = jnp.zeros((NSA_KV_HEADS * n_rows, CMP_HIDDEN), F32)
        for l in range(half):
            xl = src[pl.ds(l, n_rows, stride=CMP_STRIDE), :]
            xs = jnp.concatenate([xl[:, g * HEAD_DIM:(g + 1) * HEAD_DIM]
                                  for g in range(NSA_KV_HEADS)], axis=0)
            xa = (xs + pe_ref[l:l + 1, :]).astype(BF16)
            xb = (xs + pe_ref[half + l:half + l + 1, :]).astype(BF16)
            wa = w1_ref[l * HEAD_DIM:(l + 1) * HEAD_DIM, :].astype(BF16)
            wb = w1_ref[(half + l) * HEAD_DIM:(half + l + 1) * HEAD_DIM, :].astype(BF16)
            acc_a = acc_a + jnp.dot(xa, wa, preferred_element_type=F32)
            acc_b = acc_b + jnp.dot(xb, wb, preferred_element_type=F32)
        w2 = w2_ref[...].astype(BF16)
        for g in range(NSA_KV_HEADS):
            a = acc_a[g * n_rows:(g + 1) * n_rows]
            b = acc_b[g * n_rows:(g + 1) * n_rows]
            hid = a + pltpu.roll(b, n_rows - 1, 0)
            act = jax.nn.gelu(hid).astype(BF16)
            if out_ref is kc_ref:
                out = jnp.dot(act, w2, preferred_element_type=F32)
                out_ref[:, g * HEAD_DIM:(g + 1) * HEAD_DIM] = out.astype(out_ref.dtype)
            else:
                out_ref[g] = _qk(w2, act).astype(out_ref.dtype)


def _compress(hck, hcv, pe_k, w1_k, w2_k, pe_v, w1_v, w2_v, batch, seq):
    n_rows = seq // CMP_STRIDE
    w2_vt = w2_v.T
    full = lambda shape: pl.BlockSpec(shape, lambda b: (0,) * len(shape))
    kv_spec = pl.BlockSpec((seq, NSA_KV_WIDTH), lambda b: (b, 0))
    return pl.pallas_call(
        _compress_kernel,
        grid=(batch,),
        in_specs=[kv_spec, kv_spec,
                  full(pe_k.shape), full(w1_k.shape), full(w2_k.shape),
                  full(pe_v.shape), full(w1_v.shape), full(w2_vt.shape)],
        out_specs=[pl.BlockSpec((None, n_rows, NSA_KV_WIDTH), lambda b: (b, 0, 0)),
                   pl.BlockSpec((None, NSA_KV_HEADS, HEAD_DIM, n_rows), lambda b: (b, 0, 0, 0))],
        out_shape=[jax.ShapeDtypeStruct((batch, n_rows, NSA_KV_WIDTH), BF16),
                   jax.ShapeDtypeStruct((batch, NSA_KV_HEADS, HEAD_DIM, n_rows), BF16)],
        compiler_params=_params("parallel"),
        name="compress",
    )(hck, hcv, pe_k, w1_k, w2_k, pe_v, w1_v, w2_vt)


def _diff_kernel(q_ref, k_ref, v_ref, pos_ref, srow_ref, mask_ref, lq1_ref, lk1_ref, lq2_ref, lk2_ref,
                 g_ref, o_ref, kaug_ref, vaug_ref, s_ref, *, lam_init):
    step = q_ref.shape[0]
    seq = k_ref.shape[0]
    dv = 2 * HEAD_DIM
    n_heads = q_ref.shape[1] // LANES
    qi = pl.program_id(2)
    low = lax.broadcasted_iota(jnp.int32, (1, LANES), 1) < HEAD_DIM
    halves = (low, jnp.logical_not(low))
    head_lanes = [slice(hh * LANES, (hh + 1) * LANES) for hh in range(n_heads)]

    @pl.when(qi == 0)
    def _():
        for hh in range(n_heads):
            k = k_ref[:, head_lanes[hh]]
            for c in range(2):
                kaug_ref[hh, c] = jnp.where(halves[c], k, pos_ref[...])
            vaug_ref[hh, :, :dv] = v_ref[:, head_lanes[hh]]
            vaug_ref[hh, :, dv:] = jnp.ones((seq, dv), BF16)

    q_aug = []
    for hh in range(n_heads):
        q = q_ref[:, head_lanes[hh]]
        srow = jnp.broadcast_to(srow_ref[hh:hh + 1, :], q.shape)
        q_aug.append([jnp.where(halves[c], q, srow) for c in range(2)])
    lam = (jnp.exp(jnp.sum(lq1_ref[...] * lk1_ref[...], axis=-1, keepdims=True))
           - jnp.exp(jnp.sum(lq2_ref[...] * lk2_ref[...], axis=-1, keepdims=True)) + lam_init)

    tq = mask_ref.shape[1]
    per_chain = tq // LANES

    def sweep(step_idx):
        for r in range(step // tq):
            rows = slice(r * tq, (r + 1) * tq)
            n_tiles = (step_idx * step + (r + 1) * tq) // LANES
            masks = {n_tiles - per_chain + u: mask_ref[u] for u in range(per_chain)}
            for hh in range(n_heads):
                outs = []
                for c in range(2):
                    buf = s_ref.at[hh * 2 + c, r % DIFF_SCORE_BUFS]
                    lane_max = _stage_scores(q_aug[hh][c][rows], kaug_ref.at[hh, c], buf, 0, n_tiles, masks)
                    acc = _softmax_pv(buf, lane_max, vaug_ref.at[hh], 0, n_tiles)
                    outs.append(acc[:, :dv] / acc[:, dv:])
                o = outs[0] - lam * outs[1]
                o = o * lax.rsqrt(jnp.mean(o * o, axis=-1, keepdims=True) + RMS_EPS) * g_ref[...]
                o_ref[rows, head_lanes[hh]] = (o * (1.0 - lam_init)).astype(o_ref.dtype)

    for step_idx in range(seq // step):
        pl.when(qi == step_idx)(functools.partial(sweep, step_idx))


def _diff_attn(h3, lq1, lk1, lq2, lk2, subln_g, slopes):
    batch, seq, _ = h3.shape
    tq = DIFF_STEP
    assert seq % tq == 0 and tq % DIFF_TQ == 0 and DIFF_TQ % LANES == 0
    assert DIFF_HEADS % DIFF_HEADS_PER_STEP == 0 and (OFF_DK // LANES) % DIFF_HEADS_PER_STEP == 0
    kb, vb = OFF_DK // LANES, OFF_DV // LANES
    dv = 2 * HEAD_DIM

    bias = _bias_columns(seq)
    pos = np.zeros((seq, LANES), np.float32)
    srow = np.zeros((DIFF_HEADS, 1, LANES), np.float32)
    slope_cols = _slope_columns(slopes[:DIFF_HEADS])
    for base in (0, HEAD_DIM):
        pos[:, base:base + N_BIAS_COLS] = bias
        srow[:, 0, base:base + N_BIAS_COLS] = slope_cols
    masks = _causal_stack(DIFF_TQ, DIFF_TQ, DIFF_TQ)[1].reshape(DIFF_TQ, DIFF_TQ // LANES, LANES).transpose(1, 0, 2)

    hn = DIFF_HEADS_PER_STEP
    width = hn * LANES
    kb, vb = kb // hn, vb // hn
    vec = lambda n: pl.BlockSpec((1, n), lambda b, h, i: (0, 0))
    return pl.pallas_call(
        functools.partial(_diff_kernel, lam_init=_lambda_init(0)),
        grid=(batch, DIFF_HEADS // hn, seq // tq),
        in_specs=[pl.BlockSpec((None, tq, width), lambda b, h, i: (b, i, h)),
                  pl.BlockSpec((None, seq, width), lambda b, h, i: (b, 0, kb + h)),
                  pl.BlockSpec((None, seq, width), lambda b, h, i: (b, 0, vb + h)),
                  pl.BlockSpec((seq, LANES), lambda b, h, i: (0, 0)),
                  pl.BlockSpec((None, hn, LANES), lambda b, h, i: (h, 0, 0)),
                  pl.BlockSpec(masks.shape, lambda b, h, i: (0, 0, 0)),
                  vec(HEAD_DIM), vec(HEAD_DIM), vec(HEAD_DIM), vec(HEAD_DIM), vec(dv)],
        out_specs=pl.BlockSpec((None, tq, width), lambda b, h, i: (b, i, h)),
        out_shape=jax.ShapeDtypeStruct((batch, seq, DIFF_WIDTH), BF16),
        scratch_shapes=[pltpu.VMEM((hn, 2, seq, LANES), BF16),
                        pltpu.VMEM((hn, seq, 2 * dv), BF16),
                        pltpu.VMEM((hn * 2, DIFF_SCORE_BUFS, DIFF_TQ, seq), F32)],
        compiler_params=_params("parallel", "parallel", "arbitrary"),
        name="diff_attn",
    )(h3, h3, h3, jnp.asarray(pos, BF16), jnp.asarray(srow.reshape(DIFF_HEADS // hn, hn, LANES), BF16),
      jnp.asarray(masks),
      lq1, lk1, lq2, lk2, subln_g)


def _split3(x):
    hi = x.astype(BF16)
    r1 = x - hi.astype(F32)
    mid = r1.astype(BF16)
    lo = (r1 - mid.astype(F32)).astype(BF16)
    return hi, mid, lo


def _compressed_branch(q_cmp, kc_aug, vct, t0, need_rank):
    tq = q_cmp.shape[0] // NSA_GROUP
    n_rows = kc_aug.shape[0]
    n_sel = LANES // 4
    slot = lax.broadcasted_iota(jnp.int32, (n_rows, tq), 0)
    t_q = t0 + lax.broadcasted_iota(jnp.int32, (n_rows, tq), 1)
    mask_c = (slot * CMP_STRIDE + (CMP_BLOCK - 1)) <= t_q
    maskf = jnp.where(mask_c, 1.0, 0.0)

    s_t = _qk(kc_aug, q_cmp)
    ps = []
    for i in range(NSA_GROUP):
        s = jnp.where(mask_c, s_t[:, i * tq:(i + 1) * tq], NEG_INF)
        e = jnp.exp2(s - jnp.max(s, axis=0, keepdims=True))
        ps.append(e / jnp.sum(e, axis=0, keepdims=True) * maskf)
    o_t = jnp.dot(vct, jnp.concatenate(ps, axis=1).astype(BF16), preferred_element_type=F32)
    pairs = [jnp.concatenate([o_t[:, (2 * j) * tq:(2 * j + 1) * tq],
                              o_t[:, (2 * j + 1) * tq:(2 * j + 2) * tq]], axis=0).T
             for j in range(NSA_GROUP // 2)]

    blk = lax.broadcasted_iota(jnp.int32, (n_sel, tq), 0)
    cur = (t0 + lax.broadcasted_iota(jnp.int32, (n_sel, tq), 1)) // SEL_BLOCK
    future = blk > cur
    if not need_rank:
        keep = jnp.logical_not(future)
    else:
        s_i = lax.broadcasted_iota(jnp.int32, (n_sel, n_rows), 0)
        n_i = lax.broadcasted_iota(jnp.int32, (n_sel, n_rows), 1)
        ov_t = ((n_i * CMP_STRIDE <= s_i * SEL_BLOCK + (SEL_BLOCK - 1))
                & (n_i * CMP_STRIDE + (CMP_BLOCK - 1) >= s_i * SEL_BLOCK)).astype(BF16)
        psum = ps[0] + ps[1] + ps[2] + ps[3]
        imp_t = jnp.zeros((n_sel, tq), F32)
        for piece in _split3(psum):
            imp_t = imp_t + jnp.dot(ov_t, piece, preferred_element_type=F32)
        forced = (blk == 0) | (blk == cur) | (blk == cur - 1)
        val = jnp.where(forced, jnp.inf, jnp.where(future, -jnp.inf, imp_t))
        rank = jnp.zeros((n_sel, tq), F32)
        for sp in range(n_sel):
            other = val[sp:sp + 1, :]
            beats = (other > val) | ((other == val) & (sp < blk))
            rank = rank + jnp.where(beats, 1.0, 0.0)
        keep = (rank < float(SEL_TOPN)) & jnp.logical_not(future)
    return pairs, jnp.where(keep, 0.0, NEG_INF)


def _cmp_select_kernel(q_ref, kc_ref, vct_ref, ocmp_ref, selx_ref):
    tq = q_ref.shape[0]
    n_sel = LANES // 4
    qi = pl.program_id(1)
    n_plain = (SEL_TOPN * SEL_BLOCK) // tq

    def tile(need_rank):
        q = q_ref[...]
        for g in range(NSA_KV_HEADS):
            q_cmp = jnp.concatenate([q[:, i * LANES + g * HEAD_DIM:i * LANES + (g + 1) * HEAD_DIM]
                                     for i in range(NSA_GROUP)], axis=0)
            pairs, pen = _compressed_branch(q_cmp, kc_ref[:, g * HEAD_DIM:(g + 1) * HEAD_DIM], vct_ref[g],
                                            qi * tq, need_rank)
            for j in range(NSA_GROUP // 2):
                ocmp_ref[g * (NSA_GROUP // 2) + j] = pairs[j].astype(ocmp_ref.dtype)
            base = NSA_EXTRA_BASE[g]
            rows = [jnp.zeros((base, tq), F32)] if base else []
            rows += [pen, jnp.zeros((LANES - base - n_sel, tq), F32)]
            selx_ref[g] = jnp.concatenate(rows, axis=0).T.astype(selx_ref.dtype)

    pl.when(qi < n_plain)(functools.partial(tile, False))
    pl.when(qi >= n_plain)(functools.partial(tile, True))


def _cmp_select(h3, kc, vct):
    batch, seq, _ = h3.shape
    tq = CMP_TQ
    n_pairs = NSA_HEADS // 2
    assert 4 * (seq // SEL_BLOCK) == LANES and 2 * HEAD_DIM == LANES and (SEL_TOPN * SEL_BLOCK) % tq == 0
    return pl.pallas_call(
        _cmp_select_kernel,
        grid=(batch, seq // tq),
        in_specs=[pl.BlockSpec((None, tq, NSA_WIDTH), lambda b, i: (b, i, OFF_NQ // NSA_WIDTH)),
                  pl.BlockSpec((None,) + kc.shape[1:], lambda b, i: (b, 0, 0)),
                  pl.BlockSpec((None,) + vct.shape[1:], lambda b, i: (b, 0, 0, 0))],
        out_specs=[pl.BlockSpec((None, n_pairs, tq, LANES), lambda b, i: (b, 0, i, 0)),
                   pl.BlockSpec((None, NSA_KV_HEADS, tq, LANES), lambda b, i: (b, 0, i, 0))],
        out_shape=[jax.ShapeDtypeStruct((batch, n_pairs, seq, LANES), BF16),
                   jax.ShapeDtypeStruct((batch, NSA_KV_HEADS, seq, LANES), BF16)],
        compiler_params=_params("parallel", "parallel"),
        name="cmp_select",
    )(h3, kc, vct)


def _nsa_kernel(q_ref, sk_ref, sv_ref, wk_ref, wv_ref, gate_ref, selx_ref, ocmp_ref, ext_ref, srow_ref,
                mask_ref, gexp_ref, o_ref, ksel_ref, vsel_ref, kwin_ref, vwin_ref, ssel_ref, swin_ref):
    step = q_ref.shape[0]
    seq = sk_ref.shape[0]
    tq = ATT_TQ
    qi = pl.program_id(1)
    win_tiles = WINDOW // tq + 1
    lane = lax.broadcasted_iota(jnp.int32, (1, LANES), 1)
    halves = (lane < HEAD_DIM, lane >= HEAD_DIM)

    @pl.when(qi == 0)
    def _():
        for g in range(NSA_KV_HEADS):
            ksel_ref[g] = jnp.where(halves[g], sk_ref[...], ext_ref[g])
            kwin_ref[g] = jnp.where(halves[g], wk_ref[...], ext_ref[g])
            vsel_ref[g] = jnp.where(halves[g], sv_ref[...], jnp.ones((seq, LANES), BF16))
            vwin_ref[g] = jnp.where(halves[g], wv_ref[...], jnp.ones((seq, LANES), BF16))

    def group_sweep(step_idx, g, carry):
        half = (lane // HEAD_DIM) == g
        is_g0 = g == 0
        q = q_ref[...]
        pen = selx_ref[g]
        diag, tail = mask_ref[0], mask_ref[1]
        gsig = jax.nn.sigmoid(gate_ref[...].astype(F32)).astype(BF16)
        for r in range(step // tq):
            rsl = slice(r * tq, (r + 1) * tq)
            q_sel, q_win = [], []
            for i in range(NSA_GROUP):
                blk = q[rsl, i * LANES:(i + 1) * LANES]
                srow = jnp.broadcast_to(srow_ref[g * NSA_GROUP + i], blk.shape)
                q_sel.append(jnp.where(half, blk, pen[rsl] + srow))
                q_win.append(jnp.where(half, blk, srow))
            q_sel = jnp.concatenate(q_sel, axis=0)
            q_win = jnp.concatenate(q_win, axis=0)

            n_tiles = (step_idx * step + (r + 1) * tq) // LANES
            sbuf, wbuf = ssel_ref.at[r % SCORE_BUFS], swin_ref.at[r % SCORE_BUFS]
            if n_tiles < win_tiles:
                w_start, w_tiles, w_masks = 0, n_tiles, {n_tiles - 1: diag}
            else:
                w_start, w_tiles = (n_tiles - win_tiles) * LANES, win_tiles
                w_masks = {0: tail, win_tiles - 1: diag}
            max_s = _stage_scores(q_sel, ksel_ref.at[g], sbuf, 0, n_tiles, {n_tiles - 1: diag})
            max_w = _stage_scores(q_win, kwin_ref.at[g], wbuf, w_start, w_tiles, w_masks)
            acc_s = _softmax_pv(sbuf, max_s, vsel_ref.at[g], 0, n_tiles)
            acc_w = _softmax_pv(wbuf, max_w, vwin_ref.at[g], w_start, w_tiles)
            o_sel = acc_s / pltpu.roll(acc_s, HEAD_DIM, 1)
            o_win = acc_w / pltpu.roll(acc_w, HEAD_DIM, 1)
            gate = [jnp.dot(gsig[rsl], gexp_ref[g, br], preferred_element_type=F32) for br in range(3)]
            for j in range(NSA_GROUP // 2):
                first = slice((2 * j) * tq, (2 * j + 1) * tq)
                second = slice((2 * j + 1) * tq, (2 * j + 2) * tq)

                def head_pair(o):
                    keep = jnp.where(is_g0, o[first], o[second])
                    move = jnp.where(is_g0, o[second], o[first])
                    return jnp.where(half, keep, pltpu.roll(move, HEAD_DIM, 1))

                cols = slice(j * LANES, (j + 1) * LANES)
                blk_idx = g * (NSA_GROUP // 2) + j
                out = (gate[0][:, cols] * ocmp_ref[blk_idx, rsl, :].astype(F32)
                       + gate[1][:, cols] * head_pair(o_sel) + gate[2][:, cols] * head_pair(o_win))
                o_ref[blk_idx, rsl, :] = out.astype(o_ref.dtype)
        return carry

    for step_idx in range(seq // step):
        @pl.when(qi == step_idx)
        def _(step_idx=step_idx):
            lax.fori_loop(0, NSA_KV_HEADS, functools.partial(group_sweep, step_idx), 0)


def _nsa_attn(h3, selx, ocmp, slopes):
    batch, seq, _ = h3.shape
    tq = NSA_STEP
    assert seq % tq == 0 and tq % ATT_TQ == 0 and ATT_TQ == LANES and WINDOW % ATT_TQ == 0
    n_sel = seq // SEL_BLOCK
    rows = NSA_GROUP * ATT_TQ
    win_keys = WINDOW + ATT_TQ

    bias = _bias_columns(seq)
    onehot = (np.arange(seq)[:, None] // SEL_BLOCK == np.arange(n_sel)[None, :]).astype(np.float32)
    ext = np.zeros((NSA_KV_HEADS, seq, LANES), np.float32)
    srow = np.zeros((NSA_HEADS, 1, LANES), np.float32)
    slope_cols = _slope_columns(slopes[DIFF_HEADS:])
    for g in range(NSA_KV_HEADS):
        base = NSA_EXTRA_BASE[g]
        ext[g, :, base:base + n_sel] = onehot
        ext[g, :, base + n_sel:base + n_sel + N_BIAS_COLS] = bias
        srow[g * NSA_GROUP:(g + 1) * NSA_GROUP, 0, base + n_sel:base + n_sel + N_BIAS_COLS] = \
            slope_cols[g * NSA_GROUP:(g + 1) * NSA_GROUP]
    masks = _causal_stack(rows, LANES, ATT_TQ)[[1, 3]]
    gexp = np.zeros((NSA_KV_HEADS, 3, LANES, NSA_GROUP * HEAD_DIM), np.float32)
    for g in range(NSA_KV_HEADS):
        for br in range(3):
            for i in range(NSA_GROUP):
                gexp[g, br, 3 * (g * NSA_GROUP + i) + br, i * HEAD_DIM:(i + 1) * HEAD_DIM] = 1.0

    n_pairs = NSA_HEADS // 2
    kvspec = lambda off: pl.BlockSpec((None, seq, LANES), lambda b, i: (b, 0, off // LANES))
    tile = lambda w, blk: pl.BlockSpec((None, tq, w), lambda b, i: (b, i, blk))
    blocks = lambda n: pl.BlockSpec((None, n, tq, LANES), lambda b, i: (b, 0, i, 0))
    const = lambda a: pl.BlockSpec(a.shape, lambda b, i: (0,) * a.ndim)
    return pl.pallas_call(
        _nsa_kernel,
        grid=(batch, seq // tq),
        in_specs=[tile(NSA_WIDTH, OFF_NQ // NSA_WIDTH),
                  kvspec(OFF_SK), kvspec(OFF_SV), kvspec(OFF_WK), kvspec(OFF_WV),
                  tile(LANES, OFF_G // LANES), blocks(NSA_KV_HEADS), blocks(n_pairs),
                  const(ext), const(srow), const(masks), const(gexp)],
        out_specs=blocks(n_pairs),
        out_shape=jax.ShapeDtypeStruct((batch, n_pairs, seq, LANES), BF16),
        scratch_shapes=[pltpu.VMEM((NSA_KV_HEADS, seq, LANES), BF16) for _ in range(4)]
                       + [pltpu.VMEM((SCORE_BUFS, rows, seq), F32),
                          pltpu.VMEM((SCORE_BUFS, rows, win_keys), F32)],
        compiler_params=_params("parallel", "arbitrary"),
        name="nsa_attn",
    )(h3, h3, h3, h3, h3, h3, selx, ocmp, jnp.asarray(ext, BF16), jnp.asarray(srow, BF16),
      jnp.asarray(masks), jnp.asarray(gexp, BF16))


def _layer_norm(y, g, b):
    mu = jnp.mean(y, axis=-1, keepdims=True)
    yc = y - mu
    var = jnp.mean(yc * yc, axis=-1, keepdims=True)
    return yc * lax.rsqrt(var + LN_EPS) * g + b


def _out_ffn_kernel(od_ref, on_ref, x_ref, wo_ref, g1_ref, b1_ref, wg_ref, wu_ref, wd_ref,
                    g2_ref, b2_ref, o_ref):
    for r0 in range(0, x_ref.shape[0], FFN_CHAIN):
        rows = slice(r0, r0 + FFN_CHAIN)
        att = jnp.concatenate([od_ref[rows, :]] + [on_ref[j, rows, :] for j in range(on_ref.shape[0])], axis=1)
        mix = jnp.dot(att, wo_ref[...], preferred_element_type=F32)
        x1 = _layer_norm(DEEPNORM_ALPHA * x_ref[rows, :] + mix, g1_ref[...], b1_ref[...])
        x1b = x1.astype(BF16)
        acc = jnp.zeros(x1.shape, F32)
        for c0 in range(0, D_FF, FFN_TF):
            gate = jnp.dot(x1b, wg_ref[:, c0:c0 + FFN_TF], preferred_element_type=F32)
            up = jnp.dot(x1b, wu_ref[:, c0:c0 + FFN_TF], preferred_element_type=F32)
            act = (jax.nn.silu(gate) * up).astype(BF16)
            acc = acc + jnp.dot(act, wd_ref[c0:c0 + FFN_TF, :], preferred_element_type=F32)
        o_ref[rows, :] = _layer_norm(DEEPNORM_ALPHA * x1 + acc, g2_ref[...], b2_ref[...])


def _out_ffn(od, on, x2, wo, g1, b1, wg, wu, wd, g2, b2):
    m = x2.shape[0]
    tm = FFN_TM
    _, n_pairs, seq, _ = on.shape
    per_seq = seq // tm
    const = lambda shape: pl.BlockSpec(shape, lambda i: (0, 0), pipeline_mode=pl.Buffered(1))
    return pl.pallas_call(
        _out_ffn_kernel,
        grid=(m // tm,),
        in_specs=[pl.BlockSpec((tm, DIFF_WIDTH), lambda i: (i, 0)),
                  pl.BlockSpec((None, n_pairs, tm, LANES), lambda i: (i // per_seq, 0, i % per_seq, 0)),
                  pl.BlockSpec((tm, D_MODEL), lambda i: (i, 0)),
                  const(wo.shape), const(g1.shape), const(b1.shape),
                  const(wg.shape), const(wu.shape), const(wd.shape),
                  const(g2.shape), const(b2.shape)],
        out_specs=pl.BlockSpec((tm, D_MODEL), lambda i: (i, 0)),
        out_shape=jax.ShapeDtypeStruct((m, D_MODEL), F32),
        compiler_params=_params("parallel"),
        name="out_ffn",
    )(od, on, x2, wo, g1, b1, wg, wu, wd, g2, b2)


def kernel(x, w_in, diff_lq1, diff_lk1, diff_lq2, diff_lk2, diff_subln_g, cmp_pe_k, cmp_w1_k, cmp_w2_k,
           cmp_pe_v, cmp_w1_v, cmp_w2_v, w_out, ln1_g, ln1_b, w_gate, w_up, w_down, ln2_g, ln2_b):
    batch, seq, _ = x.shape
    assert w_in.shape[0] == DEPTH
    assert seq % DIFF_STEP == 0 and seq % NSA_STEP == 0
    assert (batch * seq) % PROJ_TM == 0 and (batch * seq) % FFN_TM == 0
    slopes = _alibi_slopes()
    x2 = x.reshape(batch * seq, D_MODEL)

    w = w_in[0]
    head_order = [g * NSA_GROUP + i for i in range(NSA_GROUP) for g in range(NSA_KV_HEADS)]
    w_nq = w[:, OFF_NQ:OFF_CK].reshape(D_MODEL, NSA_HEADS, HEAD_DIM)[:, head_order, :]
    w_pad = jnp.concatenate([w[:, :OFF_NQ], w_nq.reshape(D_MODEL, NSA_WIDTH), w[:, OFF_CK:],
                             jnp.zeros((D_MODEL, N_IN_PAD - N_IN), w.dtype)], axis=1).astype(BF16)
    h, hck, hcv = _in_proj(x2, w_pad)
    h3 = h.reshape(batch, seq, N_IN_PAD)

    kc, vct = _compress(hck, hcv, cmp_pe_k[0], cmp_w1_k[0], cmp_w2_k[0],
                        cmp_pe_v[0], cmp_w1_v[0], cmp_w2_v[0], batch, seq)
    o_diff = _diff_attn(h3, diff_lq1, diff_lk1, diff_lq2, diff_lk2, diff_subln_g, slopes)
    o_cmp, selx = _cmp_select(h3, kc, vct)
    o_nsa = _nsa_attn(h3, selx, o_cmp, slopes)

    assert seq % FFN_TM == 0
    out = _out_ffn(o_diff.reshape(batch * seq, DIFF_WIDTH), o_nsa, x2,
                   w_out[0].astype(BF16), ln1_g, ln1_b,
                   w_gate[0].astype(BF16), w_up[0].astype(BF16), w_down[0].astype(BF16), ln2_g, ln2_b)
    return out.reshape(batch, seq, D_MODEL)
```
